```python
import math
import jax, jax.numpy as jnp
from jax import lax
import numpy as np


D_MODEL = 1024
BATCH = 8
SEQ = 8192
DEPTH = 2
DEC_BATCH = 32
DEC_SEQ = 32
PAST_LEN = 1024

CHUNK = 64
N_MIXERS = 2
N_LAYERS_A = (DEPTH + 1) // 2
N_LAYERS_B = DEPTH // 2

A_HEADS = 16
A_HEAD_DIM = D_MODEL // A_HEADS
A_KV_HEADS = 4
A_GROUP = A_HEADS // A_KV_HEADS
IDX_HEADS = 8
IDX_DIM = 64
TOPK_MAX = 256
A_QBLOCK = CHUNK
A_Q_COLS = A_HEADS * A_HEAD_DIM
A_KV_COLS = A_KV_HEADS * A_HEAD_DIM
IDX_Q_COLS = IDX_HEADS * IDX_DIM
A_SPLITS = (A_Q_COLS, A_Q_COLS + A_KV_COLS, A_Q_COLS + 2 * A_KV_COLS,
            A_Q_COLS + 2 * A_KV_COLS + IDX_Q_COLS, A_Q_COLS + 2 * A_KV_COLS + IDX_Q_COLS + IDX_DIM)
A_IN_COLS = A_SPLITS[-1] + IDX_HEADS
IDX_SCALE = IDX_DIM ** -0.5 * IDX_HEADS ** -0.5

B_HEADS = 8
B_HEAD_DIM = D_MODEL // (2 * B_HEADS)
B_QBLOCK = 128
B_IN_COLS = 3 * B_HEADS * 2 * B_HEAD_DIM

D_FF = -(-8 * D_MODEL // (3 * 256)) * 256
PLE_DIM = 256

ALPHA = (2 * DEPTH) ** 0.25
BETA = (8 * DEPTH) ** -0.25
LN_EPS = 1e-5

kernel_name = "dsa_diffattn_deepnorm_streaming_step"


def alibi_slopes(n_heads):
    return jnp.asarray(2.0 ** (-8.0 * np.arange(1, n_heads + 1) / n_heads), dtype=jnp.float32)


def chunk_limit(pos):
    return (pos // CHUNK + 1) * CHUNK


def layer_norm(x, g, b):
    xf = x.astype(jnp.float32)
    mu = jnp.mean(xf, axis=-1, keepdims=True)
    var = jnp.mean(jnp.square(xf - mu), axis=-1, keepdims=True)
    return ((xf - mu) * lax.rsqrt(var + LN_EPS) * g + b).astype(x.dtype)


def rms_norm(x, g):
    xf = x.astype(jnp.float32)
    return xf * lax.rsqrt(jnp.mean(jnp.square(xf), axis=-1, keepdims=True) + LN_EPS) * g


def dsa_project(x, w_in):
    B, T, _ = x.shape
    q, k, v, qi, ki, wi = jnp.split(x @ w_in, list(A_SPLITS), axis=-1)
    q = q.reshape(B, T, A_KV_HEADS, A_GROUP, A_HEAD_DIM)
    k = k.reshape(B, T, A_KV_HEADS, A_HEAD_DIM)
    v = v.reshape(B, T, A_KV_HEADS, A_HEAD_DIM)
    qi = qi.reshape(B, T, IDX_HEADS, IDX_DIM)
    return q, k, v, qi, ki, wi


def dsa_attend(q, qi, wi, q_pos, k, v, ki, n_sel):
    B, T = q.shape[:2]
    L = k.shape[1]
    key_pos = jnp.arange(L, dtype=jnp.int32)
    limit = chunk_limit(q_pos)
    admissible = key_pos[None, :] < limit[:, None]
    dots = jnp.einsum('bthd,bsd->bths', qi, ki, preferred_element_type=jnp.float32)
    score = jnp.einsum('bths,bth->bts', jax.nn.relu(dots), wi.astype(jnp.float32) * IDX_SCALE)
    score = jnp.where(admissible[None], score, -jnp.inf)
    _, sel = lax.top_k(score, n_sel)
    valid = sel < limit[None, :, None]
    gather = jax.vmap(lambda kb, ib: jnp.take(kb, ib, axis=0))
    k_sel = gather(k, sel)
    v_sel = gather(v, sel)
    logits = jnp.einsum('btcgd,btkcd->btcgk', q, k_sel,
                        preferred_element_type=jnp.float32) * (A_HEAD_DIM ** -0.5)
    dist = jnp.abs(q_pos[None, :, None] - sel).astype(jnp.float32)
    slopes = alibi_slopes(A_HEADS).reshape(A_KV_HEADS, A_GROUP)
    logits = logits - slopes[None, None, :, :, None] * dist[:, :, None, None, :]
    logits = jnp.where(valid[:, :, None, None, :], logits, -jnp.inf)
    probs = jax.nn.softmax(logits, axis=-1).astype(v.dtype)
    out = jnp.einsum('btcgk,btkcd->btcgd', probs, v_sel)
    return out.reshape(B, T, A_HEADS * A_HEAD_DIM)


def mixer_a_prompt(x, w_in, w_out, n_sel):
    B, S, _ = x.shape
    q, k, v, qi, ki, wi = dsa_project(x, w_in)
    nb = S // A_QBLOCK

    def to_blocks(t):
        return jnp.moveaxis(t.reshape(B, nb, A_QBLOCK, *t.shape[2:]), 1, 0)

    def body(args):
        qb, qib, wib, blk = args
        q_pos = blk * A_QBLOCK + jnp.arange(A_QBLOCK, dtype=jnp.int32)
        return dsa_attend(qb, qib, wib, q_pos, k, v, ki, n_sel)

    out = lax.map(body, (to_blocks(q), to_blocks(qi), to_blocks(wi), jnp.arange(nb, dtype=jnp.int32)))
    out = jnp.moveaxis(out, 0, 1).reshape(B, S, A_Q_COLS)
    return out @ w_out, k, v, ki


def mixer_a_sample(x, cache_k, cache_v, cache_ki, w_in, w_out, n_sel):
    B, T, _ = x.shape
    P = cache_k.shape[1]
    q, k, v, qi, ki, wi = dsa_project(x, w_in)
    k_all = jnp.concatenate([cache_k, k], axis=1)
    v_all = jnp.concatenate([cache_v, v], axis=1)
    ki_all = jnp.concatenate([cache_ki, ki], axis=1)
    q_pos = P + jnp.arange(T, dtype=jnp.int32)
    out = dsa_attend(q, qi, wi, q_pos, k_all, v_all, ki_all, n_sel)
    return out @ w_out, k, v, ki


def diff_project(x, w_in):
    B, T, _ = x.shape
    q, k, v = jnp.split(x @ w_in, 3, axis=-1)
    q = q.reshape(B, T, B_HEADS, 2, B_HEAD_DIM)
    k = k.reshape(B, T, B_HEADS, 2 * B_HEAD_DIM)
    v = v.reshape(B, T, B_HEADS, 2 * B_HEAD_DIM)
    return q, k, v


def diff_attend(q, q_pos, k, v, lam):
    B, L = k.shape[:2]
    k2 = k.reshape(B, L, B_HEADS, 2, B_HEAD_DIM)
    logits = jnp.einsum('bthmd,bshmd->bhmts', q, k2,
                        preferred_element_type=jnp.float32) * (B_HEAD_DIM ** -0.5)
    key_pos = jnp.arange(L, dtype=jnp.int32)
    dist = jnp.abs(q_pos[:, None] - key_pos[None, :]).astype(jnp.float32)
    slopes = alibi_slopes(B_HEADS)
    logits = logits - slopes[None, :, None, None, None] * dist
    admissible = key_pos[None, :] < chunk_limit(q_pos)[:, None]
    logits = jnp.where(admissible, logits, -jnp.inf)
    probs = jax.nn.softmax(logits, axis=-1)
    attn = probs[:, :, 0] - lam * probs[:, :, 1]
    return jnp.einsum('bhts,bshe->bthe', attn.astype(v.dtype), v)


def diff_lambda(lam_params, lambda_init):
    lp = lam_params.astype(jnp.float32)
    return jnp.exp(jnp.sum(lp[0] * lp[1])) - jnp.exp(jnp.sum(lp[2] * lp[3])) + lambda_init


def diff_finish(out, subln_g, w_out, lambda_init, dtype):
    B, T = out.shape[:2]
    o = rms_norm(out, subln_g) * (1.0 - lambda_init)
    return o.astype(dtype).reshape(B, T, B_HEADS * 2 * B_HEAD_DIM) @ w_out


def mixer_b_prompt(x, w_in, lam_params, subln_g, w_out, lambda_init):
    B, S, _ = x.shape
    q, k, v = diff_project(x, w_in)
    lam = diff_lambda(lam_params, lambda_init)
    nb = S // B_QBLOCK
    qb = jnp.moveaxis(q.reshape(B, nb, B_QBLOCK, B_HEADS, 2, B_HEAD_DIM), 1, 0)

    def body(args):
        qblk, blk = args
        q_pos = blk * B_QBLOCK + jnp.arange(B_QBLOCK, dtype=jnp.int32)
        return diff_attend(qblk, q_pos, k, v, lam)

    out = lax.map(body, (qb, jnp.arange(nb, dtype=jnp.int32)))
    out = jnp.moveaxis(out, 0, 1).reshape(B, S, B_HEADS, 2 * B_HEAD_DIM)
    return diff_finish(out, subln_g, w_out, lambda_init, x.dtype), k, v


def mixer_b_sample(x, cache_k, cache_v, w_in, lam_params, subln_g, w_out, lambda_init):
    B, T, _ = x.shape
    P = cache_k.shape[1]
    q, k, v = diff_project(x, w_in)
    lam = diff_lambda(lam_params, lambda_init)
    k_all = jnp.concatenate([cache_k, k], axis=1)
    v_all = jnp.concatenate([cache_v, v], axis=1)
    q_pos = P + jnp.arange(T, dtype=jnp.int32)
    out = diff_attend(q, q_pos, k_all, v_all, lam)
    return diff_finish(out, subln_g, w_out, lambda_init, x.dtype), k, v


def channel_and_ple(x, p, w13, w2, w_ple, w_gate, g, b):
    gate, up = jnp.split(x @ w13, 2, axis=-1)
    ffn = (jax.nn.silu(gate) * up) @ w2
    ple = jax.nn.sigmoid(x @ w_gate) * (p @ w_ple)
    return layer_norm(ALPHA * x + ffn + ple, g, b)


def setup_inputs(seed: int = 0) -> dict:
    key = jax.random.key(seed)
    ks = jax.random.split(key, 24)
    n = jax.random.normal
    f32 = jnp.float32
    return {
        'x_prompt': n(ks[0], (BATCH, SEQ, D_MODEL), f32),
        'x_sample': n(ks[1], (DEC_BATCH, DEC_SEQ, D_MODEL), f32),
        'cache_a_k': n(ks[2], (N_LAYERS_A, DEC_BATCH, PAST_LEN, A_KV_HEADS, A_HEAD_DIM), f32),
        'cache_a_v': n(ks[3], (N_LAYERS_A, DEC_BATCH, PAST_LEN, A_KV_HEADS, A_HEAD_DIM), f32),
        'cache_a_kidx': n(ks[4], (N_LAYERS_A, DEC_BATCH, PAST_LEN, IDX_DIM), f32),
        'cache_b_k': n(ks[5], (N_LAYERS_B, DEC_BATCH, PAST_LEN, B_HEADS, 2 * B_HEAD_DIM), f32),
        'cache_b_v': n(ks[6], (N_LAYERS_B, DEC_BATCH, PAST_LEN, B_HEADS, 2 * B_HEAD_DIM), f32),
        'p_prompt': n(ks[7], (DEPTH, BATCH, SEQ, PLE_DIM), f32),
        'p_sample': n(ks[8], (DEPTH, DEC_BATCH, DEC_SEQ, PLE_DIM), f32),
        'a_w_in': n(ks[9], (N_LAYERS_A, D_MODEL, A_IN_COLS), f32) * D_MODEL ** -0.5,
        'a_w_out': n(ks[10], (N_LAYERS_A, A_Q_COLS, D_MODEL), f32) * (A_Q_COLS ** -0.5 * BETA),
        'b_w_in': n(ks[11], (N_LAYERS_B, D_MODEL, B_IN_COLS), f32) * D_MODEL ** -0.5,
        'b_lambda': n(ks[12], (N_LAYERS_B, 4, B_HEAD_DIM), f32) * 0.1,
        'b_subln': 1.0 + 0.02 * n(ks[13], (N_LAYERS_B, 2 * B_HEAD_DIM), f32),
        'b_w_out': n(ks[14], (N_LAYERS_B, 2 * B_HEADS * B_HEAD_DIM, D_MODEL), f32) * ((2 * B_HEADS * B_HEAD_DIM) ** -0.5 * BETA),
        'ffn_w13': n(ks[15], (DEPTH, D_MODEL, 2 * D_FF), f32) * D_MODEL ** -0.5,
        'ffn_w2': n(ks[16], (DEPTH, D_FF, D_MODEL), f32) * (D_FF ** -0.5 * BETA),
        'ple_w_proj': n(ks[17], (DEPTH, PLE_DIM, D_MODEL), f32) * (PLE_DIM ** -0.5 * BETA),
        'ple_w_gate': n(ks[18], (DEPTH, D_MODEL, D_MODEL), f32) * D_MODEL ** -0.5,
        'ln_gain': 1.0 + 0.02 * n(ks[19], (DEPTH, 2, D_MODEL), f32),
        'ln_bias': 0.02 * n(ks[20], (DEPTH, 2, D_MODEL), f32),
    }


def reference(x_prompt, x_sample, cache_a_k, cache_a_v, cache_a_kidx, cache_b_k, cache_b_v,
              p_prompt, p_sample, a_w_in, a_w_out, b_w_in, b_lambda, b_subln, b_w_out,
              ffn_w13, ffn_w2, ple_w_proj, ple_w_gate, ln_gain, ln_bias):
    S = x_prompt.shape[1]
    L_sample = cache_a_k.shape[2] + x_sample.shape[1]
    n_sel_prompt = min(TOPK_MAX, S // 4)
    n_sel_sample = min(TOPK_MAX, L_sample // 4)

    yp, ys = x_prompt, x_sample
    a_k_p, a_v_p, a_ki_p, a_k_s, a_v_s, a_ki_s = [], [], [], [], [], []
    b_k_p, b_v_p, b_k_s, b_v_s = [], [], [], []
    for i in range(DEPTH):
        j = i // N_MIXERS
        if i % N_MIXERS == 0:
            mp, kp, vp, kip = mixer_a_prompt(yp, a_w_in[j], a_w_out[j], n_sel_prompt)
            ms, ksm, vsm, kism = mixer_a_sample(ys, cache_a_k[j], cache_a_v[j], cache_a_kidx[j],
                                                a_w_in[j], a_w_out[j], n_sel_sample)
            a_k_p.append(kp); a_v_p.append(vp); a_ki_p.append(kip)
            a_k_s.append(ksm); a_v_s.append(vsm); a_ki_s.append(kism)
        else:
            lambda_init = 0.8 - 0.6 * math.exp(-0.3 * i)
            mp, kp, vp = mixer_b_prompt(yp, b_w_in[j], b_lambda[j], b_subln[j], b_w_out[j], lambda_init)
            ms, ksm, vsm = mixer_b_sample(ys, cache_b_k[j], cache_b_v[j], b_w_in[j], b_lambda[j],
                                          b_subln[j], b_w_out[j], lambda_init)
            b_k_p.append(kp); b_v_p.append(vp)
            b_k_s.append(ksm); b_v_s.append(vsm)
        yp = layer_norm(ALPHA * yp + mp, ln_gain[i, 0], ln_bias[i, 0])
        ys = layer_norm(ALPHA * ys + ms, ln_gain[i, 0], ln_bias[i, 0])
        yp = channel_and_ple(yp, p_prompt[i], ffn_w13[i], ffn_w2[i], ple_w_proj[i], ple_w_gate[i],
                             ln_gain[i, 1], ln_bias[i, 1])
        ys = channel_and_ple(ys, p_sample[i], ffn_w13[i], ffn_w2[i], ple_w_proj[i], ple_w_gate[i],
                             ln_gain[i, 1], ln_bias[i, 1])

    new_a_k_prompt = jnp.stack(a_k_p)
    new_a_v_prompt = jnp.stack(a_v_p)
    new_a_kidx_prompt = jnp.stack(a_ki_p)
    new_b_k_prompt = jnp.stack(b_k_p)
    new_b_v_prompt = jnp.stack(b_v_p)
    new_a_k_sample = jnp.stack(a_k_s)
    new_a_v_sample = jnp.stack(a_v_s)
    new_a_kidx_sample = jnp.stack(a_ki_s)
    new_b_k_sample = jnp.stack(b_k_s)
    new_b_v_sample = jnp.stack(b_v_s)
    return (yp, ys, new_a_k_prompt, new_a_v_prompt, new_a_kidx_prompt, new_b_k_prompt, new_b_v_prompt,
            new_a_k_sample, new_a_v_sample, new_a_kidx_sample, new_b_k_sample, new_b_v_sample)
```

```python
import functools
import math

import jax
import jax.numpy as jnp
import numpy as np
from jax import lax
from jax.experimental import pallas as pl
from jax.experimental.pallas import tpu as pltpu

F32 = jnp.float32
BF16 = jnp.bfloat16
I32 = jnp.int32

CHUNK = 64
CHUNK_SHIFT = 6
A_HEADS = 16
A_HEAD_DIM = 64
A_KV_HEADS = 4
A_GROUP = A_HEADS // A_KV_HEADS
IDX_HEADS = 8
IDX_DIM = 64
TOPK_MAX = 256
IDX_SCALE = IDX_DIM ** -0.5 * IDX_HEADS ** -0.5
B_HEADS = 8
B_HEAD_DIM = 64
DEPTH = 2
ALPHA = (2 * DEPTH) ** 0.25
LN_EPS = 1e-5

LANES = 128
VMEM_LIMIT_BYTES = 56 * 1024 * 1024

NEG_FILL = -1e30
M_INIT = -5e29
INT_MIN = -(2 ** 31)
KEY_NEG_INF = int(np.int32(np.uint32(0xFF800000) ^ np.uint32(0x7FFFFFFF)))
POS_BIG = 2 ** 30

NT_DIMS = (((1,), (1,)), ((), ()))


def _alibi_slopes(n_heads):
    return [float(2.0 ** (-8.0 * (h + 1) / n_heads)) for h in range(n_heads)]


def _compiler_params(semantics):
    return pltpu.CompilerParams(dimension_semantics=semantics,
                                vmem_limit_bytes=VMEM_LIMIT_BYTES)


def _row_tile(m, want):
    return want if m % want == 0 else m


def _proj_kernel(x_ref, *refs, out_plan):
    n_w = len(out_plan)
    w_refs, o_refs = refs[:n_w], refs[n_w:]
    x = x_ref[...].astype(BF16)
    oi = 0
    for w_ref, outs in zip(w_refs, out_plan):
        y = jnp.dot(x, w_ref[...], preferred_element_type=F32)
        for dtype, scale in outs:
            o_refs[oi][...] = (y if scale == 1.0 else y * scale).astype(dtype)
            oi += 1


def _project(x, weights, out_plan, tm=512):
    m, k = x.shape
    tm = _row_tile(m, tm)
    in_specs = [pl.BlockSpec((tm, k), lambda i: (i, 0))]
    out_shape, out_specs = [], []
    for w, outs in zip(weights, out_plan):
        n = w.shape[1]
        in_specs.append(pl.BlockSpec((k, n), lambda i: (0, 0)))
        for dtype, _ in outs:
            out_shape.append(jax.ShapeDtypeStruct((m, n), dtype))
            out_specs.append(pl.BlockSpec((tm, n), lambda i: (i, 0)))
    return pl.pallas_call(
        functools.partial(_proj_kernel, out_plan=out_plan),
        grid=(m // tm,),
        in_specs=in_specs,
        out_specs=out_specs,
        out_shape=out_shape,
        compiler_params=_compiler_params(("parallel",)),
        name="proj",
    )(x, *weights)


def _lane_fold(x, width):
    part = x[:, :LANES]
    for s in range(1, width // LANES):
        part = part + x[:, s * LANES:(s + 1) * LANES]
    return part


def _attn_a_kernel(q_ref, qi_ref, wq_ref, kiwi_ref, k_ref, v_ref, o_ref,
                   key_ref, qs_ref, qis_ref, m_ref, l_ref, acc_ref,
                   *, tq, tk, pos0, l_valid, n_sel, pos_bits):
    i = pl.program_id(1)
    q0 = pos0 + i * tq
    rows = lax.broadcasted_iota(I32, (tq, 1), 0)
    qpos = q0 + rows
    limit = jnp.minimum((lax.shift_right_logical(qpos, CHUNK_SHIFT) + 1) * CHUNK, l_valid)
    lim_max = jnp.minimum((lax.shift_right_logical(q0 + tq - 1, CHUNK_SHIFT) + 1) * CHUNK, l_valid)
    nk = (lim_max + tk - 1) // tk

    for c in range(A_KV_HEADS):
        for g in range(A_GROUP):
            h = c * A_GROUP + g
            qs_ref[c, g * tq:(g + 1) * tq, :] = q_ref[0, :, h * A_HEAD_DIM:(h + 1) * A_HEAD_DIM]
    for h in range(IDX_HEADS):
        qis_ref[h * tq:(h + 1) * tq, :] = qi_ref[0, :, h * IDX_DIM:(h + 1) * IDX_DIM]

    w_idx = wq_ref[0][:, IDX_DIM:IDX_DIM + IDX_HEADS] * IDX_SCALE

    def score_body(j, carry):
        kic = kiwi_ref[0, j][:, :IDX_DIM]
        d = lax.dot_general(qis_ref[...], kic, NT_DIMS, preferred_element_type=F32)
        sc = jnp.maximum(d[0:tq], 0.0) * w_idx[:, 0:1]
        for h in range(1, IDX_HEADS):
            sc = sc + jnp.maximum(d[h * tq:(h + 1) * tq], 0.0) * w_idx[:, h:h + 1]
        kpos = j * tk + lax.broadcasted_iota(I32, (tq, tk), 1)
        sc = jnp.where(kpos < limit, sc, -jnp.inf)
        bits = lax.bitcast_convert_type(sc, I32)
        key = bits ^ (lax.shift_right_arithmetic(bits, 31) & 0x7FFFFFFF)
        key = jnp.where(bits == INT_MIN, 0, key)
        key_ref[j] = key
        return carry

    lax.fori_loop(0, nk, score_body, 0)

    def count(pred):
        def body(j, acc):
            u = key_ref[j]
            kpos = j * tk + lax.broadcasted_iota(I32, (tq, tk), 1)
            return acc + _lane_fold(jnp.where(pred(u, kpos), 1.0, 0.0), tk)
        acc = lax.fori_loop(0, nk, body, jnp.zeros((tq, LANES), F32))
        return jnp.sum(acc, axis=1, keepdims=True)

    k_f = float(n_sel)
    c_nonneg = count(lambda u, kp: u >= 0)
    thr = jnp.where(c_nonneg >= k_f, 0, INT_MIN).astype(I32)

    def bit_body(b, thr):
        cand = thr | lax.shift_left(jnp.int32(1), 30 - b)
        c = count(lambda u, kp: u >= cand)
        return jnp.where(c >= k_f, cand, thr)

    thr = lax.fori_loop(0, 31, bit_body, thr)

    c_gt = count(lambda u, kp: u > thr)
    c_ge = count(lambda u, kp: u >= thr)
    need = k_f - c_gt
    surplus = jnp.where(((c_ge - c_gt) > need) & (thr != KEY_NEG_INF), 1.0, 0.0)
    any_surplus = jnp.sum(surplus) > 0.0

    def tie_cut():
        def body(b, x):
            cand = x | lax.shift_left(jnp.int32(1), pos_bits - 1 - b)
            c = count(lambda u, kp: (u == thr) & (kp < cand))
            return jnp.where(c < need, cand, x)
        return lax.fori_loop(0, pos_bits, body, jnp.zeros((tq, 1), I32))

    cut = lax.cond(any_surplus, tie_cut, lambda: jnp.full((tq, 1), POS_BIG, I32))

    slopes = _alibi_slopes(A_HEADS)
    m_ref[...] = jnp.full(m_ref.shape, M_INIT, F32)
    l_ref[...] = jnp.zeros(l_ref.shape, F32)
    acc_ref[...] = jnp.zeros(acc_ref.shape, F32)

    def att_body(j, carry):
        u = key_ref[j]
        kpos = j * tk + lax.broadcasted_iota(I32, (tq, tk), 1)
        sel = ((u > thr) | ((u == thr) & (kpos <= cut))) & (kpos < limit)
        dist = jnp.abs((qpos - kpos).astype(F32))
        k_all = k_ref[0, j]
        v_all = v_ref[0, j]
        for c in range(A_KV_HEADS):
            kc = k_all[:, c * A_HEAD_DIM:(c + 1) * A_HEAD_DIM]
            vc = v_all[:, c * A_HEAD_DIM:(c + 1) * A_HEAD_DIM]
            s = lax.dot_general(qs_ref[c], kc, NT_DIMS, preferred_element_type=F32)
            ps = []
            for g in range(A_GROUP):
                sl = slice(g * tq, (g + 1) * tq)
                sh = jnp.where(sel, s[sl] - slopes[c * A_GROUP + g] * dist, NEG_FILL)
                m_old = m_ref[c, sl, :]
                m_new = jnp.maximum(m_old, jnp.max(sh, axis=1, keepdims=True))
                alpha = jnp.exp(m_old - m_new)
                p = jnp.exp(sh - m_new)
                l_ref[c, sl, :] = alpha * l_ref[c, sl, :] + jnp.sum(p, axis=1, keepdims=True)
                m_ref[c, sl, :] = m_new
                acc_ref[c, sl, :] = acc_ref[c, sl, :] * alpha
                ps.append(p.astype(BF16))
            pcat = jnp.concatenate(ps, axis=0)
            acc_ref[c] += jnp.dot(pcat, vc, preferred_element_type=F32)
        return carry

    lax.fori_loop(0, nk, att_body, 0)

    for c in range(A_KV_HEADS):
        for g in range(A_GROUP):
            h = c * A_GROUP + g
            sl = slice(g * tq, (g + 1) * tq)
            o_ref[0, :, h * A_HEAD_DIM:(h + 1) * A_HEAD_DIM] = (
                acc_ref[c, sl, :] / l_ref[c, sl, :]).astype(o_ref.dtype)


def _attend_a(q, qi, kiwi_q, kiwi_k, k, v, *, tq, tk, pos0, l_valid, n_sel):
    b, t, _ = q.shape
    l = k.shape[1]
    nkc = l // tk
    pos_bits = max(1, int(math.ceil(math.log2(l + 1))))
    kiwi_k = kiwi_k.reshape(b, nkc, tk, kiwi_k.shape[-1])
    k = k.reshape(b, nkc, tk, k.shape[-1])
    v = v.reshape(b, nkc, tk, v.shape[-1])
    kern = functools.partial(_attn_a_kernel, tq=tq, tk=tk, pos0=pos0, l_valid=l_valid,
                             n_sel=n_sel, pos_bits=pos_bits)
    qcols = A_HEADS * A_HEAD_DIM
    return pl.pallas_call(
        kern,
        grid=(b, t // tq),
        in_specs=[
            pl.BlockSpec((1, tq, qcols), lambda bi, i: (bi, i, 0)),
            pl.BlockSpec((1, tq, IDX_HEADS * IDX_DIM), lambda bi, i: (bi, i, 0)),
            pl.BlockSpec((1, tq, LANES), lambda bi, i: (bi, i, 0)),
            pl.BlockSpec((1, nkc, tk, LANES), lambda bi, i: (bi, 0, 0, 0)),
            pl.BlockSpec((1, nkc, tk, A_KV_HEADS * A_HEAD_DIM), lambda bi, i: (bi, 0, 0, 0)),
            pl.BlockSpec((1, nkc, tk, A_KV_HEADS * A_HEAD_DIM), lambda bi, i: (bi, 0, 0, 0)),
        ],
        out_specs=pl.BlockSpec((1, tq, qcols), lambda bi, i: (bi, i, 0)),
        out_shape=jax.ShapeDtypeStruct((b, t, qcols), BF16),
        scratch_shapes=[
            pltpu.VMEM((nkc, tq, tk), I32),
            pltpu.VMEM((A_KV_HEADS, A_GROUP * tq, A_HEAD_DIM), BF16),
            pltpu.VMEM((IDX_HEADS * tq, IDX_DIM), BF16),
            pltpu.VMEM((A_KV_HEADS, A_GROUP * tq, 1), F32),
            pltpu.VMEM((A_KV_HEADS, A_GROUP * tq, 1), F32),
            pltpu.VMEM((A_KV_HEADS, A_GROUP * tq, A_HEAD_DIM), F32),
        ],
        compiler_params=_compiler_params(("parallel", "arbitrary")),
        name="attn_a",
    )(q, qi, kiwi_q, kiwi_k, k, v)


def _attn_b_kernel(q_ref, k_ref, v_ref, lam_ref, slope_ref, g_ref, o_ref,
                   m_ref, l_ref, acc_ref, *, tq, tk, pos0, l_valid, lambda_init):
    h = pl.program_id(1)
    i = pl.program_id(2)
    q0 = pos0 + i * tq
    rows = lax.broadcasted_iota(I32, (tq, 1), 0)
    qpos = q0 + rows
    limit = jnp.minimum((lax.shift_right_logical(qpos, CHUNK_SHIFT) + 1) * CHUNK, l_valid)
    lim_max = jnp.minimum((lax.shift_right_logical(q0 + tq - 1, CHUNK_SHIFT) + 1) * CHUNK, l_valid)
    nk = (lim_max + tk - 1) // tk
    slope = slope_ref[pl.ds(h, 1), :][:, 0:1]

    q = q_ref[0]
    qm = (q[:, :B_HEAD_DIM], q[:, B_HEAD_DIM:])

    m_ref[...] = jnp.full(m_ref.shape, M_INIT, F32)
    l_ref[...] = jnp.zeros(l_ref.shape, F32)
    acc_ref[...] = jnp.zeros(acc_ref.shape, F32)

    def body(j, carry):
        kc = k_ref[0, j]
        vc = v_ref[0, j]
        kpos = j * tk + lax.broadcasted_iota(I32, (tq, tk), 1)
        ok = kpos < limit
        bias = slope * jnp.abs((qpos - kpos).astype(F32))
        for mi in range(2):
            km = kc[:, mi * B_HEAD_DIM:(mi + 1) * B_HEAD_DIM]
            s = lax.dot_general(qm[mi], km, NT_DIMS, preferred_element_type=F32)
            s = jnp.where(ok, s - bias, NEG_FILL)
            m_old = m_ref[mi]
            m_new = jnp.maximum(m_old, jnp.max(s, axis=1, keepdims=True))
            alpha = jnp.exp(m_old - m_new)
            p = jnp.exp(s - m_new)
            l_ref[mi] = alpha * l_ref[mi] + jnp.sum(p, axis=1, keepdims=True)
            m_ref[mi] = m_new
            acc_ref[mi] = acc_ref[mi] * alpha + jnp.dot(p.astype(BF16), vc, preferred_element_type=F32)
        return carry

    lax.fori_loop(0, nk, body, 0)

    lp = lam_ref[...]
    lam = (jnp.exp(jnp.sum(lp[0:1] * lp[1:2], axis=1, keepdims=True))
           - jnp.exp(jnp.sum(lp[2:3] * lp[3:4], axis=1, keepdims=True)) + lambda_init)
    out = acc_ref[0] / l_ref[0] - lam * (acc_ref[1] / l_ref[1])
    rms = lax.rsqrt(jnp.mean(out * out, axis=1, keepdims=True) + LN_EPS)
    o_ref[0] = (out * rms * g_ref[...] * (1.0 - lambda_init)).astype(o_ref.dtype)


def _attend_b(q, k, v, lam_params, subln_g, *, tq, tk, pos0, l_valid, lambda_init):
    b, t, cols = q.shape
    l = k.shape[1]
    nkc = l // tk
    hd = 2 * B_HEAD_DIM
    k = k.reshape(b, nkc, tk, cols)
    v = v.reshape(b, nkc, tk, cols)
    slopes = jnp.broadcast_to(jnp.asarray(_alibi_slopes(B_HEADS), F32)[:, None], (B_HEADS, LANES))
    kern = functools.partial(_attn_b_kernel, tq=tq, tk=tk, pos0=pos0, l_valid=l_valid,
                             lambda_init=lambda_init)
    return pl.pallas_call(
        kern,
        grid=(b, B_HEADS, t // tq),
        in_specs=[
            pl.BlockSpec((1, tq, hd), lambda bi, h, i: (bi, i, h)),
            pl.BlockSpec((1, nkc, tk, hd), lambda bi, h, i: (bi, 0, 0, h)),
            pl.BlockSpec((1, nkc, tk, hd), lambda bi, h, i: (bi, 0, 0, h)),
            pl.BlockSpec((4, B_HEAD_DIM), lambda bi, h, i: (0, 0)),
            pl.BlockSpec((B_HEADS, LANES), lambda bi, h, i: (0, 0)),
            pl.BlockSpec((1, hd), lambda bi, h, i: (0, 0)),
        ],
        out_specs=pl.BlockSpec((1, tq, hd), lambda bi, h, i: (bi, i, h)),
        out_shape=jax.ShapeDtypeStruct((b, t, cols), BF16),
        scratch_shapes=[
            pltpu.VMEM((2, tq, 1), F32),
            pltpu.VMEM((2, tq, 1), F32),
            pltpu.VMEM((2, tq, hd), F32),
        ],
        compiler_params=_compiler_params(("parallel", "parallel", "arbitrary")),
        name="attn_b",
    )(q, k, v, lam_params, slopes, subln_g.reshape(1, hd))


def _layer_norm(z, g, b):
    mu = jnp.mean(z, axis=1, keepdims=True)
    zc = z - mu
    var = jnp.mean(zc * zc, axis=1, keepdims=True)
    return zc * lax.rsqrt(var + LN_EPS) * g + b


def _post_kernel(x_ref, a_ref, w_ref, g_ref, b_ref, o_ref):
    mix = jnp.dot(a_ref[...], w_ref[...], preferred_element_type=F32)
    o_ref[...] = _layer_norm(ALPHA * x_ref[...] + mix, g_ref[...], b_ref[...])


def _post_mixer(x, a, w_out, g, b, tm=512):
    m, d = x.shape
    tm = _row_tile(m, tm)
    return pl.pallas_call(
        _post_kernel,
        grid=(m // tm,),
        in_specs=[
            pl.BlockSpec((tm, d), lambda i: (i, 0)),
            pl.BlockSpec((tm, a.shape[1]), lambda i: (i, 0)),
            pl.BlockSpec(w_out.shape, lambda i: (0, 0)),
            pl.BlockSpec((1, d), lambda i: (0, 0)),
            pl.BlockSpec((1, d), lambda i: (0, 0)),
        ],
        out_specs=pl.BlockSpec((tm, d), lambda i: (i, 0)),
        out_shape=jax.ShapeDtypeStruct((m, d), F32),
        compiler_params=_compiler_params(("parallel",)),
        name="post_mixer",
    )(x, a, w_out, g.reshape(1, d), b.reshape(1, d))


def _sigmoid(x):
    return 1.0 / (1.0 + jnp.exp(-x))


def _ffn_kernel(x_ref, p_ref, w1_ref, w3_ref, w2_ref, wg_ref, wp_ref, g_ref, b_ref, o_ref, acc_ref):
    f = pl.program_id(1)
    x = x_ref[...]
    xb = x.astype(BF16)

    @pl.when(f == 0)
    def _():
        gate = _sigmoid(jnp.dot(xb, wg_ref[...], preferred_element_type=F32))
        ple = jnp.dot(p_ref[...].astype(BF16), wp_ref[...], preferred_element_type=F32)
        acc_ref[...] = ALPHA * x + gate * ple

    gt = jnp.dot(xb, w1_ref[...], preferred_element_type=F32)
    up = jnp.dot(xb, w3_ref[...], preferred_element_type=F32)
    hid = (gt * _sigmoid(gt)) * up
    acc_ref[...] += jnp.dot(hid.astype(BF16), w2_ref[...], preferred_element_type=F32)

    @pl.when(f == pl.num_programs(1) - 1)
    def _():
        o_ref[...] = _layer_norm(acc_ref[...], g_ref[...], b_ref[...])


def _ffn_tile(d_ff):
    for nf in (1, 2, 3, 4, 5, 6, 7, 8, 11, 22):
        if d_ff % nf == 0 and (d_ff // nf) % LANES == 0 and d_ff // nf <= 1536:
            return d_ff // nf
    return d_ff


def _channel_and_ple(x, p, w1, w3, w2, wg, wp, g, b, tm=512):
    m, d = x.shape
    tm = _row_tile(m, tm)
    d_ff = w1.shape[1]
    tf = _ffn_tile(d_ff)
    return pl.pallas_call(
        _ffn_kernel,
        grid=(m // tm, d_ff // tf),
        in_specs=[
            pl.BlockSpec((tm, d), lambda i, f: (i, 0)),
            pl.BlockSpec((tm, p.shape[1]), lambda i, f: (i, 0)),
            pl.BlockSpec((d, tf), lambda i, f: (0, f)),
            pl.BlockSpec((d, tf), lambda i, f: (0, f)),
            pl.BlockSpec((tf, d), lambda i, f: (f, 0)),
            pl.BlockSpec(wg.shape, lambda i, f: (0, 0)),
            pl.BlockSpec(wp.shape, lambda i, f: (0, 0)),
            pl.BlockSpec((1, d), lambda i, f: (0, 0)),
            pl.BlockSpec((1, d), lambda i, f: (0, 0)),
        ],
        out_specs=pl.BlockSpec((tm, d), lambda i, f: (i, 0)),
        out_shape=jax.ShapeDtypeStruct((m, d), F32),
        scratch_shapes=[pltpu.VMEM((tm, d), F32)],
        compiler_params=_compiler_params(("parallel", "arbitrary")),
        name="ffn_ple",
    )(x, p, w1, w3, w2, wg, wp, g.reshape(1, d), b.reshape(1, d))


def _pad_keys(x, mult):
    l = x.shape[1]
    lp = -(-l // mult) * mult
    if lp == l:
        return x
    return jnp.pad(x, ((0, 0), (0, lp - l)) + ((0, 0),) * (x.ndim - 2))


def _split_a_weights(w_in):
    qc = A_HEADS * A_HEAD_DIM
    kvc = A_KV_HEADS * A_HEAD_DIM
    ic = IDX_HEADS * IDX_DIM
    o = 0
    pieces = []
    for n in (qc, kvc, kvc, ic):
        pieces.append(w_in[:, o:o + n].astype(BF16))
        o += n
    tail = w_in[:, o:o + IDX_DIM + IDX_HEADS]
    tail = jnp.pad(tail, ((0, 0), (0, LANES - tail.shape[1])))
    pieces.append(tail.astype(BF16))
    return pieces


def _project_a(x2d, w_pieces):
    plan = [[(BF16, A_HEAD_DIM ** -0.5)],
            [(F32, 1.0), (BF16, 1.0)],
            [(F32, 1.0), (BF16, 1.0)],
            [(BF16, 1.0)],
            [(F32, 1.0), (BF16, 1.0)]]
    return _project(x2d, w_pieces, plan)


def _mixer_a(xp, xs, cache_k, cache_v, cache_ki, w_in):
    bp, s, d = xp.shape
    bs, t, _ = xs.shape
    past = cache_k.shape[1]
    kvc = A_KV_HEADS * A_HEAD_DIM
    w_pieces = _split_a_weights(w_in)

    q, k32, k16, v32, v16, qi, kiwi32, kiwi16 = _project_a(xp.reshape(bp * s, d), w_pieces)
    n_sel_p = min(TOPK_MAX, s // 4)
    tq_p = 128 if s % 128 == 0 else s
    tk_p = 512 if s % 512 == 0 else s
    att_p = _attend_a(q.reshape(bp, s, -1), qi.reshape(bp, s, -1), kiwi32.reshape(bp, s, LANES),
                      kiwi16.reshape(bp, s, LANES), k16.reshape(bp, s, kvc), v16.reshape(bp, s, kvc),
                      tq=tq_p, tk=tk_p, pos0=0, l_valid=s, n_sel=n_sel_p)
    new_p = (k32.reshape(bp, s, A_KV_HEADS, A_HEAD_DIM), v32.reshape(bp, s, A_KV_HEADS, A_HEAD_DIM),
             kiwi32.reshape(bp, s, LANES)[:, :, :IDX_DIM])

    qs, k32s, k16s, v32s, v16s, qis, kiwi32s, kiwi16s = _project_a(xs.reshape(bs * t, d), w_pieces)
    l_s = past + t
    n_sel_s = min(TOPK_MAX, l_s // 4)
    tk_s = 384
    k_all = _pad_keys(jnp.concatenate([cache_k.reshape(bs, past, kvc).astype(BF16),
                                       k16s.reshape(bs, t, kvc)], axis=1), tk_s)
    v_all = _pad_keys(jnp.concatenate([cache_v.reshape(bs, past, kvc).astype(BF16),
                                       v16s.reshape(bs, t, kvc)], axis=1), tk_s)
    ki_cache = jnp.pad(cache_ki.astype(BF16), ((0, 0), (0, 0), (0, LANES - IDX_DIM)))
    ki_all = _pad_keys(jnp.concatenate([ki_cache, kiwi16s.reshape(bs, t, LANES)], axis=1), tk_s)
    att_s = _attend_a(qs.reshape(bs, t, -1), qis.reshape(bs, t, -1), kiwi32s.reshape(bs, t, LANES),
                      ki_all, k_all, v_all, tq=t, tk=tk_s, pos0=past, l_valid=l_s, n_sel=n_sel_s)
    new_s = (k32s.reshape(bs, t, A_KV_HEADS, A_HEAD_DIM), v32s.reshape(bs, t, A_KV_HEADS, A_HEAD_DIM),
             kiwi32s.reshape(bs, t, LANES)[:, :, :IDX_DIM])
    return att_p.reshape(bp * s, -1), att_s.reshape(bs * t, -1), new_p, new_s


def _project_b(x2d, w_in_bf):
    cols = B_HEADS * 2 * B_HEAD_DIM
    w_pieces = [w_in_bf[:, :cols], w_in_bf[:, cols:2 * cols], w_in_bf[:, 2 * cols:]]
    plan = [[(BF16, B_HEAD_DIM ** -0.5)],
            [(F32, 1.0), (BF16, 1.0)],
            [(F32, 1.0), (BF16, 1.0)]]
    return _project(x2d, w_pieces, plan)


def _mixer_b(xp, xs, cache_k, cache_v, w_in, lam_params, subln_g, lambda_init):
    bp, s, d = xp.shape
    bs, t, _ = xs.shape
    past = cache_k.shape[1]
    cols = B_HEADS * 2 * B_HEAD_DIM
    w_in_bf = w_in.astype(BF16)

    q, k32, k16, v32, v16 = _project_b(xp.reshape(bp * s, d), w_in_bf)
    tq_p = 256 if s % 256 == 0 else s
    tk_p = 512 if s % 512 == 0 else s
    att_p = _attend_b(q.reshape(bp, s, cols), k16.reshape(bp, s, cols), v16.reshape(bp, s, cols),
                      lam_params, subln_g, tq=tq_p, tk=tk_p, pos0=0, l_valid=s, lambda_init=lambda_init)
    new_p = (k32.reshape(bp, s, B_HEADS, 2 * B_HEAD_DIM), v32.reshape(bp, s, B_HEADS, 2 * B_HEAD_DIM))

    qs, k32s, k16s, v32s, v16s = _project_b(xs.reshape(bs * t, d), w_in_bf)
    l_s = past + t
    tk_s = 384
    k_all = _pad_keys(jnp.concatenate([cache_k.reshape(bs, past, cols).astype(BF16),
                                       k16s.reshape(bs, t, cols)], axis=1), tk_s)
    v_all = _pad_keys(jnp.concatenate([cache_v.reshape(bs, past, cols).astype(BF16),
                                       v16s.reshape(bs, t, cols)], axis=1), tk_s)
    att_s = _attend_b(qs.reshape(bs, t, cols), k_all, v_all, lam_params, subln_g,
                      tq=t, tk=tk_s, pos0=past, l_valid=l_s, lambda_init=lambda_init)
    new_s = (k32s.reshape(bs, t, B_HEADS, 2 * B_HEAD_DIM), v32s.reshape(bs, t, B_HEADS, 2 * B_HEAD_DIM))
    return att_p.reshape(bp * s, cols), att_s.reshape(bs * t, cols), new_p, new_s


def kernel(x_prompt, x_sample, cache_a_k, cache_a_v, cache_a_kidx, cache_b_k, cache_b_v, p_prompt, p_sample, a_w_in, a_w_out, b_w_in, b_lambda, b_subln, b_w_out, ffn_w13, ffn_w2, ple_w_proj, ple_w_gate, ln_gain, ln_bias):
    bp, s, d = x_prompt.shape
    bs, t, _ = x_sample.shape
    depth = ffn_w13.shape[0]
    d_ff = ffn_w2.shape[1]

    yp = x_prompt.reshape(bp * s, d)
    ys = x_sample.reshape(bs * t, d)
    a_new_p, a_new_s, b_new_p, b_new_s = [], [], [], []
    for i in range(depth):
        j = i // 2
        if i % 2 == 0:
            mp, ms, new_p, new_s = _mixer_a(yp.reshape(bp, s, d), ys.reshape(bs, t, d),
                                            cache_a_k[j], cache_a_v[j], cache_a_kidx[j], a_w_in[j])
            w_out = a_w_out[j].astype(BF16)
            a_new_p.append(new_p)
            a_new_s.append(new_s)
        else:
            lambda_init = 0.8 - 0.6 * math.exp(-0.3 * i)
            mp, ms, new_p, new_s = _mixer_b(yp.reshape(bp, s, d), ys.reshape(bs, t, d),
                                            cache_b_k[j], cache_b_v[j], b_w_in[j], b_lambda[j], b_subln[j],
                                            lambda_init)
            w_out = b_w_out[j].astype(BF16)
            b_new_p.append(new_p)
            b_new_s.append(new_s)
        yp = _post_mixer(yp, mp, w_out, ln_gain[i, 0], ln_bias[i, 0])
        ys = _post_mixer(ys, ms, w_out, ln_gain[i, 0], ln_bias[i, 0])
        w13 = ffn_w13[i].astype(BF16)
        w1, w3 = w13[:, :d_ff], w13[:, d_ff:]
        w2 = ffn_w2[i].astype(BF16)
        wg = ple_w_gate[i].astype(BF16)
        wp = ple_w_proj[i].astype(BF16)
        yp = _channel_and_ple(yp, p_prompt[i].reshape(bp * s, -1), w1, w3, w2, wg, wp, ln_gain[i, 1], ln_bias[i, 1])
        ys = _channel_and_ple(ys, p_sample[i].reshape(bs * t, -1), w1, w3, w2, wg, wp, ln_gain[i, 1], ln_bias[i, 1])

    def stack(items, idx):
        return jnp.stack([it[idx] for it in items])

    return (yp.reshape(bp, s, d), ys.reshape(bs, t, d),
            stack(a_new_p, 0), stack(a_new_p, 1), stack(a_new_p, 2),
            stack(b_new_p, 0), stack(b_new_p, 1),
            stack(a_new_s, 0), stack(a_new_s, 1), stack(a_new_s, 2),
            stack(b_new_s, 0), stack(b_new_s, 1))
```

```python
import functools
import math

import jax
import jax.numpy as jnp
import numpy as np
from jax import lax
from jax.experimental import pallas as pl
from jax.experimental.pallas import tpu as pltpu

F32 = jnp.float32
BF16 = jnp.bfloat16
I32 = jnp.int32

CHUNK = 64
CHUNK_SHIFT = 6
A_HEADS = 16
A_HEAD_DIM = 64
A_KV_HEADS = 4
A_GROUP = A_HEADS // A_KV_HEADS
IDX_HEADS = 8
IDX_DIM = 64
TOPK_MAX = 256
IDX_SCALE = IDX_DIM ** -0.5 * IDX_HEADS ** -0.5
B_HEADS = 8
B_HEAD_DIM = 64
DEPTH = 2
ALPHA = (2 * DEPTH) ** 0.25
LN_EPS = 1e-5

LANES = 128
HALF = LANES // 2
VMEM_LIMIT_BYTES = 56 * 1024 * 1024

NEG_FILL = -1e30
M_INIT = -5e29
INT_MIN = -(2 ** 31)
KEY_NEG_INF = int(np.int32(np.uint32(0xFF800000) ^ np.uint32(0x7FFFFFFF)))
POS_BIG = 2 ** 30

B_ROW_GROUP = 256
LOG2E = float(np.log2(np.e))
NT_DIMS = (((1,), (1,)), ((), ()))


def _alibi_slopes(n_heads):
    return [float(2.0 ** (-8.0 * (h + 1) / n_heads)) for h in range(n_heads)]


def _compiler_params(semantics):
    return pltpu.CompilerParams(dimension_semantics=semantics,
                                vmem_limit_bytes=VMEM_LIMIT_BYTES)


def _row_tile(m, want):
    return want if m % want == 0 else m


def _block_geometry(pos0, i, tq, tk, l_valid):
    q0 = pos0 + i * tq
    qpos = q0 + lax.broadcasted_iota(I32, (tq, 1), 0)
    limit = jnp.minimum((lax.shift_right_logical(qpos, CHUNK_SHIFT) + 1) * CHUNK, l_valid)
    lim_max = jnp.minimum((lax.shift_right_logical(q0 + tq - 1, CHUNK_SHIFT) + 1) * CHUNK, l_valid)
    return qpos, limit, (lim_max + tk - 1) // tk


def _proj_kernel(x_ref, *refs, out_plan):
    n_w = len(out_plan)
    w_refs, o_refs = refs[:n_w], refs[n_w:]
    x = x_ref[...].astype(BF16)
    oi = 0
    for w_ref, outs in zip(w_refs, out_plan):
        y = jnp.dot(x, w_ref[...], preferred_element_type=F32)
        for dtype, scale in outs:
            o_refs[oi][...] = (y if scale == 1.0 else y * scale).astype(dtype)
            oi += 1


def _project(x, weights, out_plan, tm=512):
    m, k = x.shape
    tm = _row_tile(m, tm)
    in_specs = [pl.BlockSpec((tm, k), lambda i: (i, 0))]
    out_shape, out_specs = [], []
    for w, outs in zip(weights, out_plan):
        n = w.shape[1]
        in_specs.append(pl.BlockSpec((k, n), lambda i: (0, 0)))
        for dtype, _ in outs:
            out_shape.append(jax.ShapeDtypeStruct((m, n), dtype))
            out_specs.append(pl.BlockSpec((tm, n), lambda i: (i, 0)))
    return pl.pallas_call(
        functools.partial(_proj_kernel, out_plan=out_plan),
        grid=(m // tm,),
        in_specs=in_specs,
        out_specs=out_specs,
        out_shape=out_shape,
        compiler_params=_compiler_params(("parallel",)),
        name="proj",
    )(x, *weights)


def _lane_fold(x, width):
    part = x[:, :LANES]
    for s in range(1, width // LANES):
        part = part + x[:, s * LANES:(s + 1) * LANES]
    return part


def _attn_a_kernel(q_ref, qi_ref, wq_ref, kiwi_ref, k_ref, v_ref, o_ref,
                   key_ref, qs_ref, qis_ref, m_ref, acc_ref,
                   *, tq, tk, pos0, l_valid, n_sel, pos_bits):
    i = pl.program_id(1)
    nslab = tk // LANES
    qpos, limit, nk = _block_geometry(pos0, i, tq, tk, l_valid)
    lane = lax.broadcasted_iota(I32, (1, LANES), 1)

    for c in range(A_KV_HEADS):
        for g in range(A_GROUP):
            h = c * A_GROUP + g
            qs_ref[c, g * tq:(g + 1) * tq, :] = q_ref[0, :, h * LANES:(h + 1) * LANES]
    for h in range(IDX_HEADS):
        qis_ref[h * tq:(h + 1) * tq, :] = qi_ref[0, :, h * LANES:(h + 1) * LANES]

    w_idx = wq_ref[0][:, IDX_DIM:IDX_DIM + IDX_HEADS] * IDX_SCALE

    half_rows = IDX_HEADS // 2 * tq

    def score_body(j, carry):
        kic = kiwi_ref[0, j]
        ds = [lax.dot_general(qis_ref[r * half_rows:(r + 1) * half_rows, :], kic, NT_DIMS,
                              preferred_element_type=F32) for r in range(2)]
        sc = None
        for h in range(IDX_HEADS):
            d = ds[h // 4][(h % 4) * tq:(h % 4 + 1) * tq]
            term = jnp.maximum(d, 0.0) * w_idx[:, h:h + 1]
            sc = term if sc is None else sc + term
        kpos = j * tk + lax.broadcasted_iota(I32, (tq, tk), 1)
        sc = jnp.where(kpos < limit, sc, -jnp.inf)
        bits = lax.bitcast_convert_type(sc, I32)
        key = bits ^ (lax.shift_right_arithmetic(bits, 31) & 0x7FFFFFFF)
        key = jnp.where(bits == INT_MIN, 0, key)
        key_ref[j] = key
        return carry

    lax.fori_loop(0, nk, score_body, 0)

    def count(pred):
        def body(j, acc):
            u = key_ref[j]
            kpos = j * tk + lax.broadcasted_iota(I32, (tq, tk), 1)
            return acc + _lane_fold(jnp.where(pred(u, kpos), 1.0, 0.0), tk)
        acc = lax.fori_loop(0, nk, body, jnp.zeros((tq, LANES), F32))
        return jnp.sum(acc, axis=1, keepdims=True)

    k_f = float(n_sel)
    n_all = (jnp.zeros((tq, 1), I32) + nk * tk).astype(F32)
    c_nonneg = count(lambda u, kp: u >= 0)
    take = c_nonneg >= k_f
    thr0 = jnp.where(take, 0, INT_MIN).astype(I32)
    c_ge0 = jnp.where(take, c_nonneg, n_all)

    def bit_body(b, carry):
        thr, c_ge = carry
        cand = thr | lax.shift_left(jnp.int32(1), 30 - b)
        c = count(lambda u, kp: u >= cand)
        take = c >= k_f
        return jnp.where(take, cand, thr), jnp.where(take, c, c_ge)

    thr, c_ge = lax.fori_loop(0, 31, bit_body, (thr0, c_ge0))

    surplus = jnp.where((c_ge > k_f) & (thr != KEY_NEG_INF), 1.0, 0.0)
    any_surplus = jnp.sum(surplus) > 0.0

    def tie_cut():
        need = k_f - count(lambda u, kp: u > thr)

        def body(b, x):
            cand = x | lax.shift_left(jnp.int32(1), pos_bits - 1 - b)
            c = count(lambda u, kp: (u == thr) & (kp < cand))
            return jnp.where(c < need, cand, x)
        return lax.fori_loop(0, pos_bits, body, jnp.zeros((tq, 1), I32))

    cut = lax.cond(any_surplus, tie_cut, lambda: jnp.full((tq, 1), POS_BIG, I32))

    slopes2 = [s * LOG2E for s in _alibi_slopes(A_HEADS)]
    m_ref[...] = jnp.full(m_ref.shape, M_INIT, F32)
    acc_ref[...] = jnp.zeros(acc_ref.shape, F32)

    def process_chunk(j, diag):
        u = key_ref[j]
        kpos = j * tk + lax.broadcasted_iota(I32, (tq, tk), 1)
        sel = (u > thr) | ((u == thr) & (kpos <= cut))
        if diag:
            sel = sel & (kpos < limit)
        mb = jnp.where(sel, 0.0, NEG_FILL)
        base = jnp.full((1, LANES), j * tk, I32).astype(F32)

        k_all = k_ref[0, j]
        v_all = v_ref[0, j]
        lane_v = lax.broadcasted_iota(I32, (tk, LANES), 1)
        ks, vvs = [], []
        for c in range(A_KV_HEADS):
            sl = slice((c // 2) * LANES, (c // 2 + 1) * LANES)
            ks.append(k_all[:, sl])
            vg = v_all[:, sl]
            keep = (lane_v < HALF) if c % 2 == 0 else (lane_v >= HALF)
            vvs.append(jnp.where(keep, vg, jnp.ones_like(vg)))
        ss = [lax.dot_general(qs_ref[c], ks[c], NT_DIMS, preferred_element_type=F32)
              for c in range(A_KV_HEADS)]

        if diag:
            qrel = qpos - j * tk
            qrel_f = qrel.astype(F32)
            dist = [jnp.abs((qrel - (lane + sb * LANES)).astype(F32)) - qrel_f for sb in range(nslab)]

        for c in range(A_KV_HEADS):
            ps, alphas = [], []
            for g in range(A_GROUP):
                slope = slopes2[c * A_GROUP + g]
                rows = slice(g * tq, (g + 1) * tq)
                ts = []
                for sb in range(nslab):
                    sl = slice(sb * LANES, (sb + 1) * LANES)
                    if diag:
                        t = ss[c][rows, sl] - slope * dist[sb]
                    else:
                        t = ss[c][rows, sl] + slope * (lane + sb * LANES).astype(F32)
                    ts.append(t + mb[:, sl])
                c_j = slope * base
                mx = ts[0]
                for t in ts[1:]:
                    mx = jnp.maximum(mx, t)
                m_old = m_ref[c, rows, :]
                m_new = jnp.maximum(m_old, jnp.max(mx, axis=1, keepdims=True) + c_j)
                ref = m_new - c_j
                alphas.append(jnp.exp2(m_old - m_new))
                ps.append(jnp.concatenate([jnp.exp2(t - ref).astype(BF16) for t in ts], axis=1))
                m_ref[c, rows, :] = m_new
            pv = jnp.dot(jnp.concatenate(ps, axis=0), vvs[c], preferred_element_type=F32)
            acc_ref[c] = acc_ref[c] * jnp.concatenate(alphas, axis=0) + pv

    def visible_chunk(j, carry):
        process_chunk(j, diag=False)
        return carry

    lax.fori_loop(0, nk - 1, visible_chunk, 0)
    process_chunk(nk - 1, diag=True)

    for c in range(A_KV_HEADS):
        for g in range(A_GROUP):
            h = c * A_GROUP + g
            a = acc_ref[c, g * tq:(g + 1) * tq, :]
            val = a / pltpu.roll(a, HALF, axis=1)
            off = (c % 2) * HALF
            o_ref[0, :, h * A_HEAD_DIM:(h + 1) * A_HEAD_DIM] = val[:, off:off + HALF].astype(o_ref.dtype)


def _attend_a(q, qi, kiwi_q, kiwi_k, k, v, *, tq, tk, pos0, l_valid, n_sel):
    b, t, _ = q.shape
    l = k.shape[1]
    nkc = l // tk
    assert tk % tq == 0 and pos0 % tq == 0 and t % tq == 0 and l % tk == 0
    pos_bits = max(1, int(math.ceil(math.log2(l + 1))))
    kvc = A_KV_HEADS * A_HEAD_DIM
    kiwi_k = kiwi_k.reshape(b, nkc, tk, LANES)
    k = k.reshape(b, nkc, tk, kvc)
    v = v.reshape(b, nkc, tk, kvc)
    kern = functools.partial(_attn_a_kernel, tq=tq, tk=tk, pos0=pos0, l_valid=l_valid,
                             n_sel=n_sel, pos_bits=pos_bits)
    ocols = A_HEADS * A_HEAD_DIM
    return pl.pallas_call(
        kern,
        grid=(b, t // tq),
        in_specs=[
            pl.BlockSpec((1, tq, A_HEADS * LANES), lambda bi, i: (bi, i, 0)),
            pl.BlockSpec((1, tq, IDX_HEADS * LANES), lambda bi, i: (bi, i, 0)),
            pl.BlockSpec((1, tq, LANES), lambda bi, i: (bi, i, 0)),
            pl.BlockSpec((1, nkc, tk, LANES), lambda bi, i: (bi, 0, 0, 0)),
            pl.BlockSpec((1, nkc, tk, kvc), lambda bi, i: (bi, 0, 0, 0)),
            pl.BlockSpec((1, nkc, tk, kvc), lambda bi, i: (bi, 0, 0, 0)),
        ],
        out_specs=pl.BlockSpec((1, tq, ocols), lambda bi, i: (bi, i, 0)),
        out_shape=jax.ShapeDtypeStruct((b, t, ocols), BF16),
        scratch_shapes=[
            pltpu.VMEM((nkc, tq, tk), I32),
            pltpu.VMEM((A_KV_HEADS, A_GROUP * tq, LANES), BF16),
            pltpu.VMEM((IDX_HEADS * tq, LANES), BF16),
            pltpu.VMEM((A_KV_HEADS, A_GROUP * tq, LANES), F32),
            pltpu.VMEM((A_KV_HEADS, A_GROUP * tq, LANES), F32),
        ],
        compiler_params=_compiler_params(("parallel", "arbitrary")),
        name="attn_a",
    )(q, qi, kiwi_q, kiwi_k, k, v)


def _attn_b_kernel(q_ref, k_ref, v_ref, lam_ref, slope_ref, g_ref, o_ref,
                   qq_ref, m_ref, acc_ref, *, tq, tk, pos0, l_valid, lambda_init):
    h = pl.program_id(1)
    i = pl.program_id(2)
    nslab = tk // LANES
    qpos, limit, nk = _block_geometry(pos0, i, tq, tk, l_valid)
    slope2 = slope_ref[pl.ds(h, 1), :] * LOG2E
    lane = lax.broadcasted_iota(I32, (1, LANES), 1)

    q = q_ref[0]
    lane_q = lax.broadcasted_iota(I32, (tq, LANES), 1)
    zero = jnp.zeros_like(q)
    qq_ref[0:tq, :] = jnp.where(lane_q < B_HEAD_DIM, q, zero)
    qq_ref[tq:2 * tq, :] = jnp.where(lane_q >= B_HEAD_DIM, q, zero)

    m_ref[...] = jnp.full(m_ref.shape, M_INIT, F32)
    acc_ref[...] = jnp.zeros(acc_ref.shape, F32)

    def finish_rows(j, rs, ts, vv):
        c_j = slope2 * (j * tk).astype(F32)
        mx = ts[0]
        for t in ts[1:]:
            mx = jnp.maximum(mx, t)
        m_old = m_ref[rs, :]
        m_new = jnp.maximum(m_old, jnp.max(mx, axis=1, keepdims=True) + c_j)
        ref = m_new - c_j
        alpha = jnp.exp2(m_old - m_new)
        p = jnp.concatenate([jnp.exp2(t - ref).astype(BF16) for t in ts], axis=1)
        pv = jnp.dot(p, vv, preferred_element_type=F32)
        acc_ref[rs, :] = acc_ref[rs, :] * jnp.concatenate([alpha, alpha], axis=1) + pv
        m_ref[rs, :] = m_new

    def chunk_operands(j):
        vc = v_ref[0, j]
        return k_ref[0, j], jnp.concatenate([vc, jnp.ones_like(vc)], axis=1)

    rg = min(tq, B_ROW_GROUP)
    groups = [slice(r, r + rg) for r in range(0, 2 * tq, rg)]

    def visible_chunk(j, carry):
        kc, vv = chunk_operands(j)
        ss = [lax.dot_general(qq_ref[rs, :], kc, NT_DIMS, preferred_element_type=F32) for rs in groups]
        for rs, s in zip(groups, ss):
            ts = []
            for g in range(nslab):
                kb = slope2 * (lane + g * LANES).astype(F32)
                ts.append(s[:, g * LANES:(g + 1) * LANES] + kb)
            finish_rows(j, rs, ts, vv)
        return carry

    lax.fori_loop(0, nk - 1, visible_chunk, 0)

    j = nk - 1
    kc, vv = chunk_operands(j)
    qrel = qpos - j * tk
    qrel_f = qrel.astype(F32)
    lim_rel = limit - j * tk
    biases, oks = [], []
    for g in range(nslab):
        krel = lane + g * LANES
        biases.append(slope2 * (jnp.abs((qrel - krel).astype(F32)) - qrel_f))
        oks.append(krel < lim_rel)
    ss = [lax.dot_general(qq_ref[rs, :], kc, NT_DIMS, preferred_element_type=F32) for rs in groups]
    for rs, s in zip(groups, ss):
        qr = slice(rs.start % tq, rs.start % tq + rg)
        ts = [jnp.where(oks[g][qr], s[:, g * LANES:(g + 1) * LANES] - biases[g][qr], NEG_FILL)
              for g in range(nslab)]
        finish_rows(j, rs, ts, vv)

    hd = 2 * B_HEAD_DIM
    lp = lam_ref[...]
    lam = (jnp.exp(jnp.sum(lp[0:1] * lp[1:2], axis=1, keepdims=True))
           - jnp.exp(jnp.sum(lp[2:3] * lp[3:4], axis=1, keepdims=True)) + lambda_init)
    out = (acc_ref[0:tq, 0:hd] / acc_ref[0:tq, hd:2 * hd]
           - lam * (acc_ref[tq:2 * tq, 0:hd] / acc_ref[tq:2 * tq, hd:2 * hd]))
    rms = lax.rsqrt(jnp.mean(out * out, axis=1, keepdims=True) + LN_EPS)
    o_ref[0] = (out * rms * g_ref[...] * (1.0 - lambda_init)).astype(o_ref.dtype)


def _attend_b(q, k, v, lam_params, subln_g, *, tq, tk, pos0, l_valid, lambda_init):
    b, t, cols = q.shape
    l = k.shape[1]
    nkc = l // tk
    hd = 2 * B_HEAD_DIM
    assert tk % tq == 0 and pos0 % tq == 0 and t % tq == 0 and l % tk == 0
    k = k.reshape(b, nkc, tk, cols)
    v = v.reshape(b, nkc, tk, cols)
    slopes = jnp.broadcast_to(jnp.asarray(_alibi_slopes(B_HEADS), F32)[:, None], (B_HEADS, LANES))
    kern = functools.partial(_attn_b_kernel, tq=tq, tk=tk, pos0=pos0, l_valid=l_valid,
                             lambda_init=lambda_init)
    return pl.pallas_call(
        kern,
        grid=(b, B_HEADS, t // tq),
        in_specs=[
            pl.BlockSpec((1, tq, hd), lambda bi, h, i: (bi, i, h)),
            pl.BlockSpec((1, nkc, tk, hd), lambda bi, h, i: (bi, 0, 0, h)),
            pl.BlockSpec((1, nkc, tk, hd), lambda bi, h, i: (bi, 0, 0, h)),
            pl.BlockSpec((4, B_HEAD_DIM), lambda bi, h, i: (0, 0)),
            pl.BlockSpec((B_HEADS, LANES), lambda bi, h, i: (0, 0)),
            pl.BlockSpec((1, hd), lambda bi, h, i: (0, 0)),
        ],
        out_specs=pl.BlockSpec((1, tq, hd), lambda bi, h, i: (bi, i, h)),
        out_shape=jax.ShapeDtypeStruct((b, t, cols), BF16),
        scratch_shapes=[
            pltpu.VMEM((2 * tq, hd), BF16),
            pltpu.VMEM((2 * tq, LANES), F32),
            pltpu.VMEM((2 * tq, 2 * hd), F32),
        ],
        compiler_params=_compiler_params(("parallel", "parallel", "arbitrary")),
        name="attn_b",
    )(q, k, v, lam_params, slopes, subln_g.reshape(1, hd))


def _layer_norm(z, g, b):
    mu = jnp.mean(z, axis=1, keepdims=True)
    zc = z - mu
    var = jnp.mean(zc * zc, axis=1, keepdims=True)
    return zc * lax.rsqrt(var + LN_EPS) * g + b


def _post_kernel(x_ref, a_ref, w_ref, g_ref, b_ref, o_ref):
    mix = jnp.dot(a_ref[...], w_ref[...], preferred_element_type=F32)
    o_ref[...] = _layer_norm(ALPHA * x_ref[...] + mix, g_ref[...], b_ref[...])


def _post_mixer(x, a, w_out, g, b, tm=512):
    m, d = x.shape
    tm = _row_tile(m, tm)
    return pl.pallas_call(
        _post_kernel,
        grid=(m // tm,),
        in_specs=[
            pl.BlockSpec((tm, d), lambda i: (i, 0)),
            pl.BlockSpec((tm, a.shape[1]), lambda i: (i, 0)),
            pl.BlockSpec(w_out.shape, lambda i: (0, 0)),
            pl.BlockSpec((1, d), lambda i: (0, 0)),
            pl.BlockSpec((1, d), lambda i: (0, 0)),
        ],
        out_specs=pl.BlockSpec((tm, d), lambda i: (i, 0)),
        out_shape=jax.ShapeDtypeStruct((m, d), F32),
        compiler_params=_compiler_params(("parallel",)),
        name="post_mixer",
    )(x, a, w_out, g.reshape(1, d), b.reshape(1, d))


def _sigmoid(x):
    return 1.0 / (1.0 + jnp.exp(-x))


def _ffn_kernel(x_ref, p_ref, w1_ref, w3_ref, w2_ref, wg_ref, wp_ref, g_ref, b_ref, o_ref, acc_ref):
    f = pl.program_id(1)
    x = x_ref[...]
    xb = x.astype(BF16)

    @pl.when(f == 0)
    def _():
        gate = _sigmoid(jnp.dot(xb, wg_ref[...], preferred_element_type=F32))
        ple = jnp.dot(p_ref[...].astype(BF16), wp_ref[...], preferred_element_type=F32)
        acc_ref[...] = ALPHA * x + gate * ple

    gt = jnp.dot(xb, w1_ref[...], preferred_element_type=F32)
    up = jnp.dot(xb, w3_ref[...], preferred_element_type=F32)
    hid = (gt * _sigmoid(gt)) * up
    acc_ref[...] += jnp.dot(hid.astype(BF16), w2_ref[...], preferred_element_type=F32)

    @pl.when(f == pl.num_programs(1) - 1)
    def _():
        o_ref[...] = _layer_norm(acc_ref[...], g_ref[...], b_ref[...])


def _ffn_tile(d_ff):
    for nf in (1, 2, 3, 4, 5, 6, 7, 8, 11, 22):
        if d_ff % nf == 0 and (d_ff // nf) % LANES == 0 and d_ff // nf <= 1536:
            return d_ff // nf
    return d_ff


def _channel_and_ple(x, p, w1, w3, w2, wg, wp, g, b, tm=512):
    m, d = x.shape
    tm = _row_tile(m, tm)
    d_ff = w1.shape[1]
    tf = _ffn_tile(d_ff)
    return pl.pallas_call(
        _ffn_kernel,
        grid=(m // tm, d_ff // tf),
        in_specs=[
            pl.BlockSpec((tm, d), lambda i, f: (i, 0)),
            pl.BlockSpec((tm, p.shape[1]), lambda i, f: (i, 0)),
            pl.BlockSpec((d, tf), lambda i, f: (0, f)),
            pl.BlockSpec((d, tf), lambda i, f: (0, f)),
            pl.BlockSpec((tf, d), lambda i, f: (f, 0)),
            pl.BlockSpec(wg.shape, lambda i, f: (0, 0)),
            pl.BlockSpec(wp.shape, lambda i, f: (0, 0)),
            pl.BlockSpec((1, d), lambda i, f: (0, 0)),
            pl.BlockSpec((1, d), lambda i, f: (0, 0)),
        ],
        out_specs=pl.BlockSpec((tm, d), lambda i, f: (i, 0)),
        out_shape=jax.ShapeDtypeStruct((m, d), F32),
        scratch_shapes=[pltpu.VMEM((tm, d), F32)],
        compiler_params=_compiler_params(("parallel", "arbitrary")),
        name="ffn_ple",
    )(x, p, w1, w3, w2, wg, wp, g.reshape(1, d), b.reshape(1, d))


def _pad_keys(x, mult):
    l = x.shape[1]
    lp = -(-l // mult) * mult
    if lp == l:
        return x
    return jnp.pad(x, ((0, 0), (0, lp - l)) + ((0, 0),) * (x.ndim - 2))


def _head_slabs(w, n_heads, dim, high_half):
    k = w.shape[0]
    w3 = w.reshape(k, n_heads, dim)
    low = jnp.pad(w3, ((0, 0), (0, 0), (0, LANES - dim)))
    high = jnp.pad(w3, ((0, 0), (0, 0), (LANES - dim, 0)))
    sel = jnp.asarray(high_half, bool)[None, :, None]
    return jnp.where(sel, high, low).reshape(k, n_heads * LANES)


def _split_a_weights(w_in):
    qc = A_HEADS * A_HEAD_DIM
    kvc = A_KV_HEADS * A_HEAD_DIM
    ic = IDX_HEADS * IDX_DIM
    w_q, w_k, w_v = w_in[:, :qc], w_in[:, qc:qc + kvc], w_in[:, qc + kvc:qc + 2 * kvc]
    o = qc + 2 * kvc
    w_qi = w_in[:, o:o + ic]
    tail = w_in[:, o + ic:o + ic + IDX_DIM + IDX_HEADS]
    tail = jnp.pad(tail, ((0, 0), (0, LANES - tail.shape[1])))
    q_high = [(h // A_GROUP) % 2 == 1 for h in range(A_HEADS)]
    pieces = [_head_slabs(w_q, A_HEADS, A_HEAD_DIM, q_high), w_k, w_v,
              _head_slabs(w_qi, IDX_HEADS, IDX_DIM, [False] * IDX_HEADS), tail]
    return [p.astype(BF16) for p in pieces]


def _project_a(x2d, w_pieces):
    plan = [[(BF16, A_HEAD_DIM ** -0.5 * LOG2E)],
            [(F32, 1.0), (BF16, 1.0)],
            [(F32, 1.0), (BF16, 1.0)],
            [(BF16, 1.0)],
            [(F32, 1.0), (BF16, 1.0)]]
    return _project(x2d, w_pieces, plan)


def _mixer_a(xp, xs, cache_k, cache_v, cache_ki, w_in):
    bp, s, d = xp.shape
    bs, t, _ = xs.shape
    past = cache_k.shape[1]
    kvc = A_KV_HEADS * A_HEAD_DIM
    w_pieces = _split_a_weights(w_in)

    q, k32, k16, v32, v16, qi, kiwi32, kiwi16 = _project_a(xp.reshape(bp * s, d), w_pieces)
    n_sel_p = min(TOPK_MAX, s // 4)
    tq_p = 128 if s % 128 == 0 else s
    tk_p = 512 if s % 512 == 0 else s
    att_p = _attend_a(q.reshape(bp, s, -1), qi.reshape(bp, s, -1), kiwi32.reshape(bp, s, LANES),
                      kiwi16.reshape(bp, s, LANES), k16.reshape(bp, s, kvc), v16.reshape(bp, s, kvc),
                      tq=tq_p, tk=tk_p, pos0=0, l_valid=s, n_sel=n_sel_p)
    new_p = (k32.reshape(bp, s, A_KV_HEADS, A_HEAD_DIM), v32.reshape(bp, s, A_KV_HEADS, A_HEAD_DIM),
             kiwi32.reshape(bp, s, LANES)[:, :, :IDX_DIM])

    qs, k32s, k16s, v32s, v16s, qis, kiwi32s, kiwi16s = _project_a(xs.reshape(bs * t, d), w_pieces)
    l_s = past + t
    n_sel_s = min(TOPK_MAX, l_s // 4)
    tk_s = 384
    k_all = _pad_keys(jnp.concatenate([cache_k.reshape(bs, past, kvc).astype(BF16),
                                       k16s.reshape(bs, t, kvc)], axis=1), tk_s)
    v_all = _pad_keys(jnp.concatenate([cache_v.reshape(bs, past, kvc).astype(BF16),
                                       v16s.reshape(bs, t, kvc)], axis=1), tk_s)
    ki_cache = jnp.pad(cache_ki.astype(BF16), ((0, 0), (0, 0), (0, LANES - IDX_DIM)))
    ki_all = _pad_keys(jnp.concatenate([ki_cache, kiwi16s.reshape(bs, t, LANES)], axis=1), tk_s)
    att_s = _attend_a(qs.reshape(bs, t, -1), qis.reshape(bs, t, -1), kiwi32s.reshape(bs, t, LANES),
                      ki_all, k_all, v_all, tq=t, tk=tk_s, pos0=past, l_valid=l_s, n_sel=n_sel_s)
    new_s = (k32s.reshape(bs, t, A_KV_HEADS, A_HEAD_DIM), v32s.reshape(bs, t, A_KV_HEADS, A_HEAD_DIM),
             kiwi32s.reshape(bs, t, LANES)[:, :, :IDX_DIM])
    return att_p.reshape(bp * s, -1), att_s.reshape(bs * t, -1), new_p, new_s


def _project_b(x2d, w_in_bf):
    cols = B_HEADS * 2 * B_HEAD_DIM
    w_pieces = [w_in_bf[:, :cols], w_in_bf[:, cols:2 * cols], w_in_bf[:, 2 * cols:]]
    plan = [[(BF16, B_HEAD_DIM ** -0.5 * LOG2E)],
            [(F32, 1.0), (BF16, 1.0)],
            [(F32, 1.0), (BF16, 1.0)]]
    return _project(x2d, w_pieces, plan)


def _mixer_b(xp, xs, cache_k, cache_v, w_in, lam_params, subln_g, lambda_init):
    bp, s, d = xp.shape
    bs, t, _ = xs.shape
    past = cache_k.shape[1]
    cols = B_HEADS * 2 * B_HEAD_DIM
    w_in_bf = w_in.astype(BF16)

    q, k32, k16, v32, v16 = _project_b(xp.reshape(bp * s, d), w_in_bf)
    tq_p = 512 if s % 512 == 0 else s
    tk_p = 1024 if s % 1024 == 0 else s
    att_p = _attend_b(q.reshape(bp, s, cols), k16.reshape(bp, s, cols), v16.reshape(bp, s, cols),
                      lam_params, subln_g, tq=tq_p, tk=tk_p, pos0=0, l_valid=s, lambda_init=lambda_init)
    new_p = (k32.reshape(bp, s, B_HEADS, 2 * B_HEAD_DIM), v32.reshape(bp, s, B_HEADS, 2 * B_HEAD_DIM))

    qs, k32s, k16s, v32s, v16s = _project_b(xs.reshape(bs * t, d), w_in_bf)
    l_s = past + t
    tk_s = 384
    k_all = _pad_keys(jnp.concatenate([cache_k.reshape(bs, past, cols).astype(BF16),
                                       k16s.reshape(bs, t, cols)], axis=1), tk_s)
    v_all = _pad_keys(jnp.concatenate([cache_v.reshape(bs, past, cols).astype(BF16),
                                       v16s.reshape(bs, t, cols)], axis=1), tk_s)
    att_s = _attend_b(qs.reshape(bs, t, cols), k_all, v_all, lam_params, subln_g,
                      tq=t, tk=tk_s, pos0=past, l_valid=l_s, lambda_init=lambda_init)
    new_s = (k32s.reshape(bs, t, B_HEADS, 2 * B_HEAD_DIM), v32s.reshape(bs, t, B_HEADS, 2 * B_HEAD_DIM))
    return att_p.reshape(bp * s, cols), att_s.reshape(bs * t, cols), new_p, new_s


def kernel(x_prompt, x_sample, cache_a_k, cache_a_v, cache_a_kidx, cache_b_k, cache_b_v, p_prompt, p_sample, a_w_in, a_w_out, b_w_in, b_lambda, b_subln, b_w_out, ffn_w13, ffn_w2, ple_w_proj, ple_w_gate, ln_gain, ln_bias):
    bp, s, d = x_prompt.shape
    bs, t, _ = x_sample.shape
    depth = ffn_w13.shape[0]
    d_ff = ffn_w2.shape[1]

    yp = x_prompt.reshape(bp * s, d)
    ys = x_sample.reshape(bs * t, d)
    a_new_p, a_new_s, b_new_p, b_new_s = [], [], [], []
    for i in range(depth):
        j = i // 2
        if i % 2 == 0:
            mp, ms, new_p, new_s = _mixer_a(yp.reshape(bp, s, d), ys.reshape(bs, t, d),
                                            cache_a_k[j], cache_a_v[j], cache_a_kidx[j], a_w_in[j])
            w_out = a_w_out[j].astype(BF16)
            a_new_p.append(new_p)
            a_new_s.append(new_s)
        else:
            lambda_init = 0.8 - 0.6 * math.exp(-0.3 * i)
            mp, ms, new_p, new_s = _mixer_b(yp.reshape(bp, s, d), ys.reshape(bs, t, d),
                                            cache_b_k[j], cache_b_v[j], b_w_in[j], b_lambda[j], b_subln[j],
                                            lambda_init)
            w_out = b_w_out[j].astype(BF16)
            b_new_p.append(new_p)
            b_new_s.append(new_s)
        yp = _post_mixer(yp, mp, w_out, ln_gain[i, 0], ln_bias[i, 0])
        ys = _post_mixer(ys, ms, w_out, ln_gain[i, 0], ln_bias[i, 0])
        w13 = ffn_w13[i].astype(BF16)
        w1, w3 = w13[:, :d_ff], w13[:, d_ff:]
        w2 = ffn_w2[i].astype(BF16)
        wg = ple_w_gate[i].astype(BF16)
        wp = ple_w_proj[i].astype(BF16)
        yp = _channel_and_ple(yp, p_prompt[i].reshape(bp * s, -1), w1, w3, w2, wg, wp, ln_gain[i, 1], ln_bias[i, 1])
        ys = _channel_and_ple(ys, p_sample[i].reshape(bs * t, -1), w1, w3, w2, wg, wp, ln_gain[i, 1], ln_bias[i, 1])

    def stack(items, idx):
        return jnp.stack([it[idx] for it in items])

    return (yp.reshape(bp, s, d), ys.reshape(bs, t, d),
            stack(a_new_p, 0), stack(a_new_p, 1), stack(a_new_p, 2),
            stack(b_new_p, 0), stack(b_new_p, 1),
            stack(a_new_s, 0), stack(a_new_s, 1), stack(a_new_s, 2),
            stack(b_new_s, 0), stack(b_new_s, 1))
```

```python
import functools
import math

import jax
import jax.numpy as jnp
import numpy as np
from jax import lax
from jax.experimental import pallas as pl
from jax.experimental.pallas import tpu as pltpu

F32 = jnp.float32
BF16 = jnp.bfloat16
I32 = jnp.int32

CHUNK = 64
CHUNK_SHIFT = 6
A_HEADS = 16
A_HEAD_DIM = 64
A_KV_HEADS = 4
A_GROUP = A_HEADS // A_KV_HEADS
IDX_HEADS = 8
IDX_DIM = 64
TOPK_MAX = 256
IDX_SCALE = IDX_DIM ** -0.5 * IDX_HEADS ** -0.5
B_HEADS = 8
B_HEAD_DIM = 64
DEPTH = 2
ALPHA = (2 * DEPTH) ** 0.25
LN_EPS = 1e-5

LANES = 128
HALF = LANES // 2
VMEM_LIMIT_BYTES = 56 * 1024 * 1024

NEG_FILL = -1e30
M_INIT = -5e29
INT_MIN = -(2 ** 31)
KEY_NEG_INF = int(np.int32(np.uint32(0xFF800000) ^ np.uint32(0x7FFFFFFF)))
POS_BIG = 2 ** 30

B_ROW_GROUP = 256
LOG2E = float(np.log2(np.e))
NT_DIMS = (((1,), (1,)), ((), ()))


def _alibi_slopes(n_heads):
    return [float(2.0 ** (-8.0 * (h + 1) / n_heads)) for h in range(n_heads)]


def _compiler_params(semantics):
    return pltpu.CompilerParams(dimension_semantics=semantics,
                                vmem_limit_bytes=VMEM_LIMIT_BYTES)


def _row_tile(m, want):
    return want if m % want == 0 else m


def _block_geometry(pos0, i, tq, tk, l_valid):
    q0 = pos0 + i * tq
    qpos = q0 + lax.broadcasted_iota(I32, (tq, 1), 0)
    limit = jnp.minimum((lax.shift_right_logical(qpos, CHUNK_SHIFT) + 1) * CHUNK, l_valid)
    lim_max = jnp.minimum((lax.shift_right_logical(q0 + tq - 1, CHUNK_SHIFT) + 1) * CHUNK, l_valid)
    return qpos, limit, (lim_max + tk - 1) // tk


def _proj_kernel(x_ref, *refs, out_plan):
    n_w = len(out_plan)
    w_refs, o_refs = refs[:n_w], refs[n_w:]
    x = x_ref[...].astype(BF16)
    oi = 0
    for w_ref, outs in zip(w_refs, out_plan):
        y = jnp.dot(x, w_ref[...], preferred_element_type=F32)
        for dtype, scale in outs:
            o_refs[oi][...] = (y if scale == 1.0 else y * scale).astype(dtype)
            oi += 1


def _project(x, weights, out_plan, tm=512):
    m, k = x.shape
    tm = _row_tile(m, tm)
    in_specs = [pl.BlockSpec((tm, k), lambda i: (i, 0))]
    out_shape, out_specs = [], []
    for w, outs in zip(weights, out_plan):
        n = w.shape[1]
        in_specs.append(pl.BlockSpec((k, n), lambda i: (0, 0)))
        for dtype, _ in outs:
            out_shape.append(jax.ShapeDtypeStruct((m, n), dtype))
            out_specs.append(pl.BlockSpec((tm, n), lambda i: (i, 0)))
    return pl.pallas_call(
        functools.partial(_proj_kernel, out_plan=out_plan),
        grid=(m // tm,),
        in_specs=in_specs,
        out_specs=out_specs,
        out_shape=out_shape,
        compiler_params=_compiler_params(("parallel",)),
        name="proj",
    )(x, *weights)


def _lane_fold(x, width):
    part = x[:, :LANES]
    for s in range(1, width // LANES):
        part = part + x[:, s * LANES:(s + 1) * LANES]
    return part


def _attn_a_kernel(q_ref, qi_ref, wq_ref, kiwi_ref, k_ref, v_ref, o_ref,
                   key_ref, hi_ref, lo_ref, qs_ref, qis_ref, m_ref, acc_ref,
                   *, tq, tk, pos0, l_valid, n_sel, pos_bits):
    i = pl.program_id(1)
    nslab = tk // LANES
    qpos, limit, nk = _block_geometry(pos0, i, tq, tk, l_valid)
    lane = lax.broadcasted_iota(I32, (1, LANES), 1)

    for c in range(A_KV_HEADS):
        for g in range(A_GROUP):
            h = c * A_GROUP + g
            qs_ref[c, g * tq:(g + 1) * tq, :] = q_ref[0, :, h * LANES:(h + 1) * LANES]
    for h in range(IDX_HEADS):
        qis_ref[h * tq:(h + 1) * tq, :] = qi_ref[0, :, h * LANES:(h + 1) * LANES]

    w_idx = wq_ref[0][:, IDX_DIM:IDX_DIM + IDX_HEADS] * IDX_SCALE

    half_rows = IDX_HEADS // 2 * tq

    def score_body(j, carry):
        kic = kiwi_ref[0, j]
        ds = [lax.dot_general(qis_ref[r * half_rows:(r + 1) * half_rows, :], kic, NT_DIMS,
                              preferred_element_type=F32) for r in range(2)]
        sc = None
        for h in range(IDX_HEADS):
            d = ds[h // 4][(h % 4) * tq:(h % 4 + 1) * tq]
            term = jnp.maximum(d, 0.0) * w_idx[:, h:h + 1]
            sc = term if sc is None else sc + term
        kpos = j * tk + lax.broadcasted_iota(I32, (tq, tk), 1)
        sc = jnp.where(kpos < limit, sc, -jnp.inf)
        bits = lax.bitcast_convert_type(sc, I32)
        key = bits ^ (lax.shift_right_arithmetic(bits, 31) & 0x7FFFFFFF)
        key = jnp.where(bits == INT_MIN, 0, key)
        key_ref[j] = key
        hi_ref[j] = lax.shift_right_arithmetic(key, 16).astype(jnp.int16)
        lo_ref[j] = ((key & 0xFFFF) - 2 ** 15).astype(jnp.int16)
        return carry

    lax.fori_loop(0, nk, score_body, 0)

    def count(pred):
        def body(j, acc):
            u = key_ref[j]
            kpos = j * tk + lax.broadcasted_iota(I32, (tq, tk), 1)
            return acc + _lane_fold(jnp.where(pred(u, kpos), 1.0, 0.0), tk)
        acc = lax.fori_loop(0, nk, body, jnp.zeros((tq, LANES), F32))
        return jnp.sum(acc, axis=1, keepdims=True)

    one16, zero16 = jnp.int16(1), jnp.int16(0)

    def count16(ref, cands):
        def body(j, accs):
            u = ref[j]
            return tuple(a + _lane_fold(jnp.where(u >= cd, one16, zero16), tk) for a, cd in zip(accs, cands))
        accs = lax.fori_loop(0, nk, body, tuple(jnp.zeros((tq, LANES), jnp.int16) for _ in cands))
        return [jnp.sum(a.astype(F32), axis=1, keepdims=True) for a in accs]

    def radix16(ref, k_need, c_all):
        def body(it, carry):
            p, c_p = carry
            step = lax.shift_left(jnp.int32(1), 14 - 2 * it)
            cps = [p + 3 * step, p + 2 * step, p + step]
            cs = count16(ref, [(cp - 2 ** 15).astype(jnp.int16) for cp in cps])
            for cp, c in zip(reversed(cps), reversed(cs)):
                take = c >= k_need
                p, c_p = jnp.where(take, cp, p), jnp.where(take, c, c_p)
            return p, c_p
        return lax.fori_loop(0, 8, body, (jnp.zeros((tq, 1), I32), c_all))

    k_f = float(n_sel)
    n_all = (jnp.zeros((tq, 1), I32) + nk * tk).astype(F32)
    p_hi, c_ge_hi = radix16(hi_ref, k_f, n_all)
    thr_hi = (p_hi - 2 ** 15).astype(jnp.int16)
    c_gt_hi = count16(hi_ref, [(p_hi + 1 - 2 ** 15).astype(jnp.int16)])[0]
    c_gt_hi = jnp.where(p_hi == 2 ** 16 - 1, 0.0, c_gt_hi)

    def restrict_body(j, carry):
        lo_ref[j] = jnp.where(hi_ref[j] == thr_hi, lo_ref[j], jnp.int16(-2 ** 15))
        return carry

    lax.fori_loop(0, nk, restrict_body, 0)
    p_lo, c_ge_lo = radix16(lo_ref, k_f - c_gt_hi, c_ge_hi - c_gt_hi)
    thr = lax.shift_left(p_hi - 2 ** 15, 16) | p_lo
    c_ge = c_gt_hi + c_ge_lo

    surplus = jnp.where((c_ge > k_f) & (thr != KEY_NEG_INF), 1.0, 0.0)
    any_surplus = jnp.sum(surplus) > 0.0

    def tie_cut():
        need = k_f - count(lambda u, kp: u > thr)

        def body(b, x):
            cand = x | lax.shift_left(jnp.int32(1), pos_bits - 1 - b)
            c = count(lambda u, kp: (u == thr) & (kp < cand))
            return jnp.where(c < need, cand, x)
        return lax.fori_loop(0, pos_bits, body, jnp.zeros((tq, 1), I32))

    cut = lax.cond(any_surplus, tie_cut, lambda: jnp.full((tq, 1), POS_BIG, I32))

    slopes2 = [s * LOG2E for s in _alibi_slopes(A_HEADS)]
    m_ref[...] = jnp.full(m_ref.shape, M_INIT, F32)
    acc_ref[...] = jnp.zeros(acc_ref.shape, F32)

    def process_chunk(j, diag):
        u = key_ref[j]
        kpos = j * tk + lax.broadcasted_iota(I32, (tq, tk), 1)
        sel = (u > thr) | ((u == thr) & (kpos <= cut))
        if diag:
            sel = sel & (kpos < limit)
        mb = jnp.where(sel, 0.0, NEG_FILL)
        base = jnp.full((1, LANES), j * tk, I32).astype(F32)

        k_all = k_ref[0, j]
        v_all = v_ref[0, j]
        lane_v = lax.broadcasted_iota(I32, (tk, LANES), 1)
        ks, vvs = [], []
        for c in range(A_KV_HEADS):
            sl = slice((c // 2) * LANES, (c // 2 + 1) * LANES)
            ks.append(k_all[:, sl])
            vg = v_all[:, sl]
            keep = (lane_v < HALF) if c % 2 == 0 else (lane_v >= HALF)
            vvs.append(jnp.where(keep, vg, jnp.ones_like(vg)))
        ss = [lax.dot_general(qs_ref[c], ks[c], NT_DIMS, preferred_element_type=F32)
              for c in range(A_KV_HEADS)]

        if diag:
            qrel = qpos - j * tk
            qrel_f = qrel.astype(F32)
            dist = [jnp.abs((qrel - (lane + sb * LANES)).astype(F32)) - qrel_f for sb in range(nslab)]

        for c in range(A_KV_HEADS):
            ps, alphas = [], []
            for g in range(A_GROUP):
                slope = slopes2[c * A_GROUP + g]
                rows = slice(g * tq, (g + 1) * tq)
                ts = []
                for sb in range(nslab):
                    sl = slice(sb * LANES, (sb + 1) * LANES)
                    if diag:
                        t = ss[c][rows, sl] - slope * dist[sb]
                    else:
                        t = ss[c][rows, sl] + slope * (lane + sb * LANES).astype(F32)
                    ts.append(t + mb[:, sl])
                c_j = slope * base
                mx = ts[0]
                for t in ts[1:]:
                    mx = jnp.maximum(mx, t)
                m_old = m_ref[c, rows, :]
                m_new = jnp.maximum(m_old, jnp.max(mx, axis=1, keepdims=True) + c_j)
                ref = m_new - c_j
                alphas.append(jnp.exp2(m_old - m_new))
                ps.append(jnp.concatenate([jnp.exp2(t - ref).astype(BF16) for t in ts], axis=1))
                m_ref[c, rows, :] = m_new
            pv = jnp.dot(jnp.concatenate(ps, axis=0), vvs[c], preferred_element_type=F32)
            acc_ref[c] = acc_ref[c] * jnp.concatenate(alphas, axis=0) + pv

    def visible_chunk(j, carry):
        process_chunk(j, diag=False)
        return carry

    lax.fori_loop(0, nk - 1, visible_chunk, 0)
    process_chunk(nk - 1, diag=True)

    for c in range(A_KV_HEADS):
        for g in range(A_GROUP):
            h = c * A_GROUP + g
            a = acc_ref[c, g * tq:(g + 1) * tq, :]
            val = a / pltpu.roll(a, HALF, axis=1)
            off = (c % 2) * HALF
            o_ref[0, :, h * A_HEAD_DIM:(h + 1) * A_HEAD_DIM] = val[:, off:off + HALF].astype(o_ref.dtype)


def _attend_a(q, qi, kiwi_q, kiwi_k, k, v, *, tq, tk, pos0, l_valid, n_sel):
    b, t, _ = q.shape
    l = k.shape[1]
    nkc = l // tk
    assert tk % tq == 0 and pos0 % tq == 0 and t % tq == 0 and l % tk == 0
    pos_bits = max(1, int(math.ceil(math.log2(l + 1))))
    kvc = A_KV_HEADS * A_HEAD_DIM
    kiwi_k = kiwi_k.reshape(b, nkc, tk, LANES)
    k = k.reshape(b, nkc, tk, kvc)
    v = v.reshape(b, nkc, tk, kvc)
    kern = functools.partial(_attn_a_kernel, tq=tq, tk=tk, pos0=pos0, l_valid=l_valid,
                             n_sel=n_sel, pos_bits=pos_bits)
    ocols = A_HEADS * A_HEAD_DIM
    return pl.pallas_call(
        kern,
        grid=(b, t // tq),
        in_specs=[
            pl.BlockSpec((1, tq, A_HEADS * LANES), lambda bi, i: (bi, i, 0)),
            pl.BlockSpec((1, tq, IDX_HEADS * LANES), lambda bi, i: (bi, i, 0)),
            pl.BlockSpec((1, tq, LANES), lambda bi, i: (bi, i, 0)),
            pl.BlockSpec((1, nkc, tk, LANES), lambda bi, i: (bi, 0, 0, 0)),
            pl.BlockSpec((1, nkc, tk, kvc), lambda bi, i: (bi, 0, 0, 0)),
            pl.BlockSpec((1, nkc, tk, kvc), lambda bi, i: (bi, 0, 0, 0)),
        ],
        out_specs=pl.BlockSpec((1, tq, ocols), lambda bi, i: (bi, i, 0)),
        out_shape=jax.ShapeDtypeStruct((b, t, ocols), BF16),
        scratch_shapes=[
            pltpu.VMEM((nkc, tq, tk), I32),
            pltpu.VMEM((nkc, tq, tk), jnp.int16),
            pltpu.VMEM((nkc, tq, tk), jnp.int16),
            pltpu.VMEM((A_KV_HEADS, A_GROUP * tq, LANES), BF16),
            pltpu.VMEM((IDX_HEADS * tq, LANES), BF16),
            pltpu.VMEM((A_KV_HEADS, A_GROUP * tq, LANES), F32),
            pltpu.VMEM((A_KV_HEADS, A_GROUP * tq, LANES), F32),
        ],
        compiler_params=_compiler_params(("parallel", "arbitrary")),
        name="attn_a",
    )(q, qi, kiwi_q, kiwi_k, k, v)


def _attn_b_kernel(q_ref, k_ref, v_ref, lam_ref, slope_ref, g_ref, o_ref,
                   qq_ref, m_ref, acc_ref, *, tq, tk, pos0, l_valid, lambda_init):
    h = pl.program_id(1)
    i = pl.program_id(2)
    nslab = tk // LANES
    qpos, limit, nk = _block_geometry(pos0, i, tq, tk, l_valid)
    slope2 = slope_ref[pl.ds(h, 1), :] * LOG2E
    lane = lax.broadcasted_iota(I32, (1, LANES), 1)

    q = q_ref[0]
    lane_q = lax.broadcasted_iota(I32, (tq, LANES), 1)
    zero = jnp.zeros_like(q)
    qq_ref[0:tq, :] = jnp.where(lane_q < B_HEAD_DIM, q, zero)
    qq_ref[tq:2 * tq, :] = jnp.where(lane_q >= B_HEAD_DIM, q, zero)

    m_ref[...] = jnp.full(m_ref.shape, M_INIT, F32)
    acc_ref[...] = jnp.zeros(acc_ref.shape, F32)

    def finish_rows(j, rs, ts, vv):
        c_j = slope2 * (j * tk).astype(F32)
        mx = ts[0]
        for t in ts[1:]:
            mx = jnp.maximum(mx, t)
        m_old = m_ref[rs, :]
        m_new = jnp.maximum(m_old, jnp.max(mx, axis=1, keepdims=True) + c_j)
        ref = m_new - c_j
        alpha = jnp.exp2(m_old - m_new)
        p = jnp.concatenate([jnp.exp2(t - ref).astype(BF16) for t in ts], axis=1)
        pv = jnp.dot(p, vv, preferred_element_type=F32)
        acc_ref[rs, :] = acc_ref[rs, :] * jnp.concatenate([alpha, alpha], axis=1) + pv
        m_ref[rs, :] = m_new

    def chunk_operands(j):
        vc = v_ref[0, j]
        return k_ref[0, j], jnp.concatenate([vc, jnp.ones_like(vc)], axis=1)

    rg = min(tq, B_ROW_GROUP)
    groups = [slice(r, r + rg) for r in range(0, 2 * tq, rg)]

    def visible_chunk(j, carry):
        kc, vv = chunk_operands(j)
        ss = [lax.dot_general(qq_ref[rs, :], kc, NT_DIMS, preferred_element_type=F32) for rs in groups]
        for rs, s in zip(groups, ss):
            ts = []
            for g in range(nslab):
                kb = slope2 * (lane + g * LANES).astype(F32)
                ts.append(s[:, g * LANES:(g + 1) * LANES] + kb)
            finish_rows(j, rs, ts, vv)
        return carry

    lax.fori_loop(0, nk - 1, visible_chunk, 0)

    j = nk - 1
    kc, vv = chunk_operands(j)
    qrel = qpos - j * tk
    qrel_f = qrel.astype(F32)
    lim_rel = limit - j * tk
    biases, oks = [], []
    for g in range(nslab):
        krel = lane + g * LANES
        biases.append(slope2 * (jnp.abs((qrel - krel).astype(F32)) - qrel_f))
        oks.append(krel < lim_rel)
    ss = [lax.dot_general(qq_ref[rs, :], kc, NT_DIMS, preferred_element_type=F32) for rs in groups]
    for rs, s in zip(groups, ss):
        qr = slice(rs.start % tq, rs.start % tq + rg)
        ts = [jnp.where(oks[g][qr], s[:, g * LANES:(g + 1) * LANES] - biases[g][qr], NEG_FILL)
              for g in range(nslab)]
        finish_rows(j, rs, ts, vv)

    hd = 2 * B_HEAD_DIM
    lp = lam_ref[...]
    lam = (jnp.exp(jnp.sum(lp[0:1] * lp[1:2], axis=1, keepdims=True))
           - jnp.exp(jnp.sum(lp[2:3] * lp[3:4], axis=1, keepdims=True)) + lambda_init)
    out = (acc_ref[0:tq, 0:hd] / acc_ref[0:tq, hd:2 * hd]
           - lam * (acc_ref[tq:2 * tq, 0:hd] / acc_ref[tq:2 * tq, hd:2 * hd]))
    rms = lax.rsqrt(jnp.mean(out * out, axis=1, keepdims=True) + LN_EPS)
    o_ref[0] = (out * rms * g_ref[...] * (1.0 - lambda_init)).astype(o_ref.dtype)


def _attend_b(q, k, v, lam_params, subln_g, *, tq, tk, pos0, l_valid, lambda_init):
    b, t, cols = q.shape
    l = k.shape[1]
    nkc = l // tk
    hd = 2 * B_HEAD_DIM
    assert tk % tq == 0 and pos0 % tq == 0 and t % tq == 0 and l % tk == 0
    k = k.reshape(b, nkc, tk, cols)
    v = v.reshape(b, nkc, tk, cols)
    slopes = jnp.broadcast_to(jnp.asarray(_alibi_slopes(B_HEADS), F32)[:, None], (B_HEADS, LANES))
    kern = functools.partial(_attn_b_kernel, tq=tq, tk=tk, pos0=pos0, l_valid=l_valid,
                             lambda_init=lambda_init)
    return pl.pallas_call(
        kern,
        grid=(b, B_HEADS, t // tq),
        in_specs=[
            pl.BlockSpec((1, tq, hd), lambda bi, h, i: (bi, i, h)),
            pl.BlockSpec((1, nkc, tk, hd), lambda bi, h, i: (bi, 0, 0, h)),
            pl.BlockSpec((1, nkc, tk, hd), lambda bi, h, i: (bi, 0, 0, h)),
            pl.BlockSpec((4, B_HEAD_DIM), lambda bi, h, i: (0, 0)),
            pl.BlockSpec((B_HEADS, LANES), lambda bi, h, i: (0, 0)),
            pl.BlockSpec((1, hd), lambda bi, h, i: (0, 0)),
        ],
        out_specs=pl.BlockSpec((1, tq, hd), lambda bi, h, i: (bi, i, h)),
        out_shape=jax.ShapeDtypeStruct((b, t, cols), BF16),
        scratch_shapes=[
            pltpu.VMEM((2 * tq, hd), BF16),
            pltpu.VMEM((2 * tq, LANES), F32),
            pltpu.VMEM((2 * tq, 2 * hd), F32),
        ],
        compiler_params=_compiler_params(("parallel", "parallel", "arbitrary")),
        name="attn_b",
    )(q, k, v, lam_params, slopes, subln_g.reshape(1, hd))


def _layer_norm(z, g, b):
    mu = jnp.mean(z, axis=1, keepdims=True)
    zc = z - mu
    var = jnp.mean(zc * zc, axis=1, keepdims=True)
    return zc * lax.rsqrt(var + LN_EPS) * g + b


def _post_kernel(x_ref, a_ref, w_ref, g_ref, b_ref, o_ref):
    mix = jnp.dot(a_ref[...], w_ref[...], preferred_element_type=F32)
    o_ref[...] = _layer_norm(ALPHA * x_ref[...] + mix, g_ref[...], b_ref[...])


def _post_mixer(x, a, w_out, g, b, tm=512):
    m, d = x.shape
    tm = _row_tile(m, tm)
    return pl.pallas_call(
        _post_kernel,
        grid=(m // tm,),
        in_specs=[
            pl.BlockSpec((tm, d), lambda i: (i, 0)),
            pl.BlockSpec((tm, a.shape[1]), lambda i: (i, 0)),
            pl.BlockSpec(w_out.shape, lambda i: (0, 0)),
            pl.BlockSpec((1, d), lambda i: (0, 0)),
            pl.BlockSpec((1, d), lambda i: (0, 0)),
        ],
        out_specs=pl.BlockSpec((tm, d), lambda i: (i, 0)),
        out_shape=jax.ShapeDtypeStruct((m, d), F32),
        compiler_params=_compiler_params(("parallel",)),
        name="post_mixer",
    )(x, a, w_out, g.reshape(1, d), b.reshape(1, d))


def _sigmoid(x):
    return 1.0 / (1.0 + jnp.exp(-x))


def _ffn_kernel(x_ref, p_ref, w1_ref, w3_ref, w2_ref, wg_ref, wp_ref, g_ref, b_ref, o_ref, acc_ref):
    f = pl.program_id(1)
    x = x_ref[...]
    xb = x.astype(BF16)

    @pl.when(f == 0)
    def _():
        gate = _sigmoid(jnp.dot(xb, wg_ref[...], preferred_element_type=F32))
        ple = jnp.dot(p_ref[...].astype(BF16), wp_ref[...], preferred_element_type=F32)
        acc_ref[...] = ALPHA * x + gate * ple

    gt = jnp.dot(xb, w1_ref[...], preferred_element_type=F32)
    up = jnp.dot(xb, w3_ref[...], preferred_element_type=F32)
    hid = (gt * _sigmoid(gt)) * up
    acc_ref[...] += jnp.dot(hid.astype(BF16), w2_ref[...], preferred_element_type=F32)

    @pl.when(f == pl.num_programs(1) - 1)
    def _():
        o_ref[...] = _layer_norm(acc_ref[...], g_ref[...], b_ref[...])


def _ffn_tile(d_ff):
    for nf in (1, 2, 3, 4, 5, 6, 7, 8, 11, 22):
        if d_ff % nf == 0 and (d_ff // nf) % LANES == 0 and d_ff // nf <= 1536:
            return d_ff // nf
    return d_ff


def _channel_and_ple(x, p, w1, w3, w2, wg, wp, g, b, tm=512):
    m, d = x.shape
    tm = _row_tile(m, tm)
    d_ff = w1.shape[1]
    tf = _ffn_tile(d_ff)
    return pl.pallas_call(
        _ffn_kernel,
        grid=(m // tm, d_ff // tf),
        in_specs=[
            pl.BlockSpec((tm, d), lambda i, f: (i, 0)),
            pl.BlockSpec((tm, p.shape[1]), lambda i, f: (i, 0)),
            pl.BlockSpec((d, tf), lambda i, f: (0, f)),
            pl.BlockSpec((d, tf), lambda i, f: (0, f)),
            pl.BlockSpec((tf, d), lambda i, f: (f, 0)),
            pl.BlockSpec(wg.shape, lambda i, f: (0, 0)),
            pl.BlockSpec(wp.shape, lambda i, f: (0, 0)),
            pl.BlockSpec((1, d), lambda i, f: (0, 0)),
            pl.BlockSpec((1, d), lambda i, f: (0, 0)),
        ],
        out_specs=pl.BlockSpec((tm, d), lambda i, f: (i, 0)),
        out_shape=jax.ShapeDtypeStruct((m, d), F32),
        scratch_shapes=[pltpu.VMEM((tm, d), F32)],
        compiler_params=_compiler_params(("parallel", "arbitrary")),
        name="ffn_ple",
    )(x, p, w1, w3, w2, wg, wp, g.reshape(1, d), b.reshape(1, d))


def _pad_keys(x, mult):
    l = x.shape[1]
    lp = -(-l // mult) * mult
    if lp == l:
        return x
    return jnp.pad(x, ((0, 0), (0, lp - l)) + ((0, 0),) * (x.ndim - 2))


def _head_slabs(w, n_heads, dim, high_half):
    k = w.shape[0]
    w3 = w.reshape(k, n_heads, dim)
    low = jnp.pad(w3, ((0, 0), (0, 0), (0, LANES - dim)))
    high = jnp.pad(w3, ((0, 0), (0, 0), (LANES - dim, 0)))
    sel = jnp.asarray(high_half, bool)[None, :, None]
    return jnp.where(sel, high, low).reshape(k, n_heads * LANES)


def _split_a_weights(w_in):
    qc = A_HEADS * A_HEAD_DIM
    kvc = A_KV_HEADS * A_HEAD_DIM
    ic = IDX_HEADS * IDX_DIM
    w_q, w_k, w_v = w_in[:, :qc], w_in[:, qc:qc + kvc], w_in[:, qc + kvc:qc + 2 * kvc]
    o = qc + 2 * kvc
    w_qi = w_in[:, o:o + ic]
    tail = w_in[:, o + ic:o + ic + IDX_DIM + IDX_HEADS]
    tail = jnp.pad(tail, ((0, 0), (0, LANES - tail.shape[1])))
    q_high = [(h // A_GROUP) % 2 == 1 for h in range(A_HEADS)]
    pieces = [_head_slabs(w_q, A_HEADS, A_HEAD_DIM, q_high), w_k, w_v,
              _head_slabs(w_qi, IDX_HEADS, IDX_DIM, [False] * IDX_HEADS), tail]
    return [p.astype(BF16) for p in pieces]


def _project_a(x2d, w_pieces):
    plan = [[(BF16, A_HEAD_DIM ** -0.5 * LOG2E)],
            [(F32, 1.0), (BF16, 1.0)],
            [(F32, 1.0), (BF16, 1.0)],
            [(BF16, 1.0)],
            [(F32, 1.0), (BF16, 1.0)]]
    return _project(x2d, w_pieces, plan)


def _mixer_a(xp, xs, cache_k, cache_v, cache_ki, w_in):
    bp, s, d = xp.shape
    bs, t, _ = xs.shape
    past = cache_k.shape[1]
    kvc = A_KV_HEADS * A_HEAD_DIM
    w_pieces = _split_a_weights(w_in)

    q, k32, k16, v32, v16, qi, kiwi32, kiwi16 = _project_a(xp.reshape(bp * s, d), w_pieces)
    n_sel_p = min(TOPK_MAX, s // 4)
    tq_p = 128 if s % 128 == 0 else s
    tk_p = 512 if s % 512 == 0 else s
    att_p = _attend_a(q.reshape(bp, s, -1), qi.reshape(bp, s, -1), kiwi32.reshape(bp, s, LANES),
                      kiwi16.reshape(bp, s, LANES), k16.reshape(bp, s, kvc), v16.reshape(bp, s, kvc),
                      tq=tq_p, tk=tk_p, pos0=0, l_valid=s, n_sel=n_sel_p)
    new_p = (k32.reshape(bp, s, A_KV_HEADS, A_HEAD_DIM), v32.reshape(bp, s, A_KV_HEADS, A_HEAD_DIM),
             kiwi32.reshape(bp, s, LANES)[:, :, :IDX_DIM])

    qs, k32s, k16s, v32s, v16s, qis, kiwi32s, kiwi16s = _project_a(xs.reshape(bs * t, d), w_pieces)
    l_s = past + t
    n_sel_s = min(TOPK_MAX, l_s // 4)
    tk_s = 384
    k_all = _pad_keys(jnp.concatenate([cache_k.reshape(bs, past, kvc).astype(BF16),
                                       k16s.reshape(bs, t, kvc)], axis=1), tk_s)
    v_all = _pad_keys(jnp.concatenate([cache_v.reshape(bs, past, kvc).astype(BF16),
                                       v16s.reshape(bs, t, kvc)], axis=1), tk_s)
    ki_cache = jnp.pad(cache_ki.astype(BF16), ((0, 0), (0, 0), (0, LANES - IDX_DIM)))
    ki_all = _pad_keys(jnp.concatenate([ki_cache, kiwi16s.reshape(bs, t, LANES)], axis=1), tk_s)
    att_s = _attend_a(qs.reshape(bs, t, -1), qis.reshape(bs, t, -1), kiwi32s.reshape(bs, t, LANES),
                      ki_all, k_all, v_all, tq=t, tk=tk_s, pos0=past, l_valid=l_s, n_sel=n_sel_s)
    new_s = (k32s.reshape(bs, t, A_KV_HEADS, A_HEAD_DIM), v32s.reshape(bs, t, A_KV_HEADS, A_HEAD_DIM),
             kiwi32s.reshape(bs, t, LANES)[:, :, :IDX_DIM])
    return att_p.reshape(bp * s, -1), att_s.reshape(bs * t, -1), new_p, new_s


def _project_b(x2d, w_in_bf):
    cols = B_HEADS * 2 * B_HEAD_DIM
    w_pieces = [w_in_bf[:, :cols], w_in_bf[:, cols:2 * cols], w_in_bf[:, 2 * cols:]]
    plan = [[(BF16, B_HEAD_DIM ** -0.5 * LOG2E)],
            [(F32, 1.0), (BF16, 1.0)],
            [(F32, 1.0), (BF16, 1.0)]]
    return _project(x2d, w_pieces, plan)


def _mixer_b(xp, xs, cache_k, cache_v, w_in, lam_params, subln_g, lambda_init):
    bp, s, d = xp.shape
    bs, t, _ = xs.shape
    past = cache_k.shape[1]
    cols = B_HEADS * 2 * B_HEAD_DIM
    w_in_bf = w_in.astype(BF16)

    q, k32, k16, v32, v16 = _project_b(xp.reshape(bp * s, d), w_in_bf)
    tq_p = 512 if s % 512 == 0 else s
    tk_p = 1024 if s % 1024 == 0 else s
    att_p = _attend_b(q.reshape(bp, s, cols), k16.reshape(bp, s, cols), v16.reshape(bp, s, cols),
                      lam_params, subln_g, tq=tq_p, tk=tk_p, pos0=0, l_valid=s, lambda_init=lambda_init)
    new_p = (k32.reshape(bp, s, B_HEADS, 2 * B_HEAD_DIM), v32.reshape(bp, s, B_HEADS, 2 * B_HEAD_DIM))

    qs, k32s, k16s, v32s, v16s = _project_b(xs.reshape(bs * t, d), w_in_bf)
    l_s = past + t
    tk_s = 384
    k_all = _pad_keys(jnp.concatenate([cache_k.reshape(bs, past, cols).astype(BF16),
                                       k16s.reshape(bs, t, cols)], axis=1), tk_s)
    v_all = _pad_keys(jnp.concatenate([cache_v.reshape(bs, past, cols).astype(BF16),
                                       v16s.reshape(bs, t, cols)], axis=1), tk_s)
    att_s = _attend_b(qs.reshape(bs, t, cols), k_all, v_all, lam_params, subln_g,
                      tq=t, tk=tk_s, pos0=past, l_valid=l_s, lambda_init=lambda_init)
    new_s = (k32s.reshape(bs, t, B_HEADS, 2 * B_HEAD_DIM), v32s.reshape(bs, t, B_HEADS, 2 * B_HEAD_DIM))
    return att_p.reshape(bp * s, cols), att_s.reshape(bs * t, cols), new_p, new_s


def kernel(x_prompt, x_sample, cache_a_k, cache_a_v, cache_a_kidx, cache_b_k, cache_b_v, p_prompt, p_sample, a_w_in, a_w_out, b_w_in, b_lambda, b_subln, b_w_out, ffn_w13, ffn_w2, ple_w_proj, ple_w_gate, ln_gain, ln_bias):
    bp, s, d = x_prompt.shape
    bs, t, _ = x_sample.shape
    depth = ffn_w13.shape[0]
    d_ff = ffn_w2.shape[1]

    yp = x_prompt.reshape(bp * s, d)
    ys = x_sample.reshape(bs * t, d)
    a_new_p, a_new_s, b_new_p, b_new_s = [], [], [], []
    for i in range(depth):
        j = i // 2
        if i % 2 == 0:
            mp, ms, new_p, new_s = _mixer_a(yp.reshape(bp, s, d), ys.reshape(bs, t, d),
                                            cache_a_k[j], cache_a_v[j], cache_a_kidx[j], a_w_in[j])
            w_out = a_w_out[j].astype(BF16)
            a_new_p.append(new_p)
            a_new_s.append(new_s)
        else:
            lambda_init = 0.8 - 0.6 * math.exp(-0.3 * i)
            mp, ms, new_p, new_s = _mixer_b(yp.reshape(bp, s, d), ys.reshape(bs, t, d),
                                            cache_b_k[j], cache_b_v[j], b_w_in[j], b_lambda[j], b_subln[j],
                                            lambda_init)
            w_out = b_w_out[j].astype(BF16)
            b_new_p.append(new_p)
            b_new_s.append(new_s)
        yp = _post_mixer(yp, mp, w_out, ln_gain[i, 0], ln_bias[i, 0])
        ys = _post_mixer(ys, ms, w_out, ln_gain[i, 0], ln_bias[i, 0])
        w13 = ffn_w13[i].astype(BF16)
        w1, w3 = w13[:, :d_ff], w13[:, d_ff:]
        w2 = ffn_w2[i].astype(BF16)
        wg = ple_w_gate[i].astype(BF16)
        wp = ple_w_proj[i].astype(BF16)
        yp = _channel_and_ple(yp, p_prompt[i].reshape(bp * s, -1), w1, w3, w2, wg, wp, ln_gain[i, 1], ln_bias[i, 1])
        ys = _channel_and_ple(ys, p_sample[i].reshape(bs * t, -1), w1, w3, w2, wg, wp, ln_gain[i, 1], ln_bias[i, 1])

    def stack(items, idx):
        return jnp.stack([it[idx] for it in items])

    return (yp.reshape(bp, s, d), ys.reshape(bs, t, d),
            stack(a_new_p, 0), stack(a_new_p, 1), stack(a_new_p, 2),
            stack(b_new_p, 0), stack(b_new_p, 1),
            stack(a_new_s, 0), stack(a_new_s, 1), stack(a_new_s, 2),
            stack(b_new_s, 0), stack(b_new_s, 1))
```

```python
import functools
import math

import jax
import jax.numpy as jnp
import ml_dtypes
import numpy as np
from jax import lax
from jax.experimental import pallas as pl
from jax.experimental.pallas import tpu as pltpu

F32 = jnp.float32
BF16 = jnp.bfloat16
I32 = jnp.int32

CHUNK = 64
CHUNK_SHIFT = 6
A_HEADS = 16
A_HEAD_DIM = 64
A_KV_HEADS = 4
A_GROUP = A_HEADS // A_KV_HEADS
IDX_HEADS = 8
IDX_DIM = 64
TOPK_MAX = 256
IDX_SCALE = IDX_DIM ** -0.5 * IDX_HEADS ** -0.5
B_HEADS = 8
B_HEAD_DIM = 64
DEPTH = 2
ALPHA = (2 * DEPTH) ** 0.25
LN_EPS = 1e-5

LANES = 128
HALF = LANES // 2
VMEM_LIMIT_BYTES = 56 * 1024 * 1024

NEG_FILL = -1e30
M_INIT = -5e29
INT_MIN = -(2 ** 31)
KEY_NEG_INF = int(np.int32(np.uint32(0xFF800000) ^ np.uint32(0x7FFFFFFF)))
POS_BIG = 2 ** 30

B_ROW_GROUP = 256
COUNT_ROWS = 64
BITS_PER_CHECK = 4
LOG2E = float(np.log2(np.e))
NT_DIMS = (((1,), (1,)), ((), ()))


def _alibi_slopes(n_heads):
    return [float(2.0 ** (-8.0 * (h + 1) / n_heads)) for h in range(n_heads)]


def _compiler_params(semantics):
    return pltpu.CompilerParams(dimension_semantics=semantics,
                                vmem_limit_bytes=VMEM_LIMIT_BYTES)


def _row_tile(m, want):
    return want if m % want == 0 else m


def _block_geometry(pos0, i, tq, tk, l_valid):
    q0 = pos0 + i * tq
    qpos = q0 + lax.broadcasted_iota(I32, (tq, 1), 0)
    limit = jnp.minimum((lax.shift_right_logical(qpos, CHUNK_SHIFT) + 1) * CHUNK, l_valid)
    lim_max = jnp.minimum((lax.shift_right_logical(q0 + tq - 1, CHUNK_SHIFT) + 1) * CHUNK, l_valid)
    return qpos, limit, (lim_max + tk - 1) // tk


def _proj_kernel(x_ref, *refs, out_plan):
    n_w = len(out_plan)
    w_refs, o_refs = refs[:n_w], refs[n_w:]
    x = x_ref[...].astype(BF16)
    oi = 0
    for w_ref, outs in zip(w_refs, out_plan):
        y = jnp.dot(x, w_ref[...], preferred_element_type=F32)
        for dtype, scale in outs:
            o_refs[oi][...] = (y if scale == 1.0 else y * scale).astype(dtype)
            oi += 1


def _project(x, weights, out_plan, tm=512):
    m, k = x.shape
    tm = _row_tile(m, tm)
    in_specs = [pl.BlockSpec((tm, k), lambda i: (i, 0))]
    out_shape, out_specs = [], []
    for w, outs in zip(weights, out_plan):
        n = w.shape[1]
        in_specs.append(pl.BlockSpec((k, n), lambda i: (0, 0)))
        for dtype, _ in outs:
            out_shape.append(jax.ShapeDtypeStruct((m, n), dtype))
            out_specs.append(pl.BlockSpec((tm, n), lambda i: (i, 0)))
    return pl.pallas_call(
        functools.partial(_proj_kernel, out_plan=out_plan),
        grid=(m // tm,),
        in_specs=in_specs,
        out_specs=out_specs,
        out_shape=out_shape,
        compiler_params=_compiler_params(("parallel",)),
        name="proj",
    )(x, *weights)


def _row_fold(x, rows):
    part = x[0:rows]
    for s in range(1, x.shape[0] // rows):
        part = part + x[s * rows:(s + 1) * rows]
    return part


def _split_bf16(x):
    hi = float(np.float32(x).astype(ml_dtypes.bfloat16))
    return hi, x - hi


def _attn_a_kernel(q_ref, qi_ref, wq_ref, kiwi_ref, k_ref, v_ref, o_ref,
                   key_ref, pf_ref, qs_ref, qis_ref, m_ref, acc_ref,
                   *, tq, tk, pos0, l_valid, t_valid, n_sel, pos_bits):
    i = pl.program_id(1)
    nslab = tk // LANES
    qpos, limit, nk = _block_geometry(pos0, i, tq, tk, l_valid)
    lane = lax.broadcasted_iota(I32, (1, LANES), 1)
    qpos_l = pos0 + i * tq + lax.broadcasted_iota(I32, (1, tq), 1)
    limit_l = jnp.minimum((lax.shift_right_logical(qpos_l, CHUNK_SHIFT) + 1) * CHUNK, l_valid)
    row_ok = lax.broadcasted_iota(I32, (1, tq), 1) < t_valid
    slopes2 = [s * LOG2E for s in _alibi_slopes(A_HEADS)]

    eye = jnp.where(lax.broadcasted_iota(I32, (tq, LANES), 0) == lax.broadcasted_iota(I32, (tq, LANES), 1),
                    1.0, 0.0).astype(BF16)
    lane_q = lax.broadcasted_iota(I32, (tq, LANES), 1)
    for c in range(A_KV_HEADS):
        fb = (1 - c % 2) * HALF
        for g in range(A_GROUP):
            h = c * A_GROUP + g
            s_hi, s_lo = _split_bf16(slopes2[h])
            coef = jnp.where(lane_q == fb, 32.0 * s_hi,
                             jnp.where(lane_q == fb + 1, s_hi,
                                       jnp.where(lane_q == fb + 2, 32.0 * s_lo, s_lo))).astype(BF16)
            is_coef = (lane_q >= fb) & (lane_q < fb + 4)
            rows = slice(g * tq, (g + 1) * tq)
            qs_ref[c, rows, 0:LANES] = jnp.where(is_coef, coef, q_ref[0, :, h * LANES:(h + 1) * LANES])
            qs_ref[c, rows, LANES:2 * LANES] = eye
    for h in range(IDX_HEADS):
        qis_ref[h * tq:(h + 1) * tq, :] = qi_ref[0, :, h * LANES:(h + 1) * LANES]
    krow = lax.broadcasted_iota(I32, (tk, LANES), 0)
    r_hi = lax.shift_right_logical(krow, 5).astype(F32)
    r_lo = (krow & 31).astype(F32)
    lane_k = lax.broadcasted_iota(I32, (tk, LANES), 1)
    for par in range(2):
        fb = (1 - par) * HALF
        pf_ref[par] = jnp.where((lane_k == fb) | (lane_k == fb + 2), r_hi,
                                jnp.where((lane_k == fb + 1) | (lane_k == fb + 3), r_lo, 0.0)).astype(BF16)

    w_t = jnp.transpose(wq_ref[0])[IDX_DIM:IDX_DIM + IDX_HEADS] * IDX_SCALE

    half_rows = IDX_HEADS // 2 * tq

    def score_body(j, carry):
        kic = kiwi_ref[0, j]
        ds = [lax.dot_general(kic, qis_ref[r * half_rows:(r + 1) * half_rows, :], NT_DIMS,
                              preferred_element_type=F32) for r in range(2)]
        sc = None
        for h in range(IDX_HEADS):
            d = ds[h // 4][:, (h % 4) * tq:(h % 4 + 1) * tq]
            term = jnp.maximum(d, 0.0) * w_t[h:h + 1]
            sc = term if sc is None else sc + term
        kpos = j * tk + lax.broadcasted_iota(I32, (tk, tq), 0)
        sc = jnp.where(kpos < limit_l, sc, -jnp.inf)
        bits = lax.bitcast_convert_type(sc, I32)
        key = bits ^ (lax.shift_right_arithmetic(bits, 31) & 0x7FFFFFFF)
        key_ref[j] = jnp.where(bits == INT_MIN, 0, key)
        return carry

    lax.fori_loop(0, nk, score_body, 0)

    def count(pred):
        def body(j, acc):
            kpos = j * tk + lax.broadcasted_iota(I32, (tk, tq), 0)
            return acc + _row_fold(jnp.where(pred(key_ref[j], kpos), 1.0, 0.0), COUNT_ROWS)
        acc = lax.fori_loop(0, nk, body, jnp.zeros((COUNT_ROWS, tq), F32))
        return jnp.sum(acc, axis=0, keepdims=True)

    k_f = float(n_sel)
    n_all = (jnp.zeros((1, tq), I32) + nk * tk).astype(F32)
    c_nonneg = count(lambda u, kp: u >= 0)
    take = c_nonneg >= k_f
    thr0 = jnp.where(take, 0, INT_MIN).astype(I32)
    c_ge0 = jnp.where(take, c_nonneg, n_all)

    def unsettled(c_ge):
        return jnp.sum(jnp.where((c_ge != k_f) & row_ok, 1.0, 0.0)) > 0.0

    def bits_cond(carry):
        grp, _, c_ge = carry
        return (grp * BITS_PER_CHECK < 31) & unsettled(c_ge)

    def bits_body(carry):
        grp, thr, c_ge = carry
        for bb in range(BITS_PER_CHECK):
            b = grp * BITS_PER_CHECK + bb
            bit = jnp.where(b <= 30, lax.shift_left(jnp.int32(1), 30 - jnp.minimum(b, 30)), 0)
            cand = thr | bit
            c = count(lambda u, kp: u >= cand)
            take = c >= k_f
            thr, c_ge = jnp.where(take, cand, thr), jnp.where(take, c, c_ge)
        return grp + 1, thr, c_ge

    _, thr, c_ge = lax.while_loop(bits_cond, bits_body, (jnp.int32(0), thr0, c_ge0))

    surplus = jnp.where((c_ge > k_f) & (thr != KEY_NEG_INF) & row_ok, 1.0, 0.0)
    any_surplus = jnp.sum(surplus) > 0.0

    def tie_cut():
        need = k_f - count(lambda u, kp: u > thr)

        def body(b, x):
            cand = x | lax.shift_left(jnp.int32(1), pos_bits - 1 - b)
            c = count(lambda u, kp: (u == thr) & (kp < cand))
            return jnp.where(c < need, cand, x)
        return lax.fori_loop(0, pos_bits, body, jnp.zeros((1, tq), I32))

    cut = lax.cond(any_surplus, tie_cut, lambda: jnp.full((1, tq), POS_BIG, I32))

    m_ref[...] = jnp.full(m_ref.shape, M_INIT, F32)
    acc_ref[...] = jnp.zeros(acc_ref.shape, F32)

    def process_chunk(j, diag):
        u = key_ref[j]
        kpos = j * tk + lax.broadcasted_iota(I32, (tk, tq), 0)
        sel = (u > thr) | ((u == thr) & (kpos <= cut))
        if diag:
            sel = sel & (kpos < limit_l)
        mask_t = jnp.where(sel, 0.0, NEG_FILL).astype(BF16)
        base = jnp.full((1, LANES), j * tk, I32).astype(F32)

        k_all = k_ref[0, j]
        v_all = v_ref[0, j]
        rhs, vvs = [], []
        for c in range(A_KV_HEADS):
            sl = slice((c // 2) * LANES, (c // 2 + 1) * LANES)
            keep = (lane_k < HALF) if c % 2 == 0 else (lane_k >= HALF)
            rhs.append(jnp.concatenate([jnp.where(keep, k_all[:, sl], pf_ref[c % 2]), mask_t], axis=1))
            vg = v_all[:, sl]
            vvs.append(jnp.where(keep, vg, jnp.ones_like(vg)))
        ss = [lax.dot_general(qs_ref[c], rhs[c], NT_DIMS, preferred_element_type=F32)
              for c in range(A_KV_HEADS)]

        if diag:
            qrel = qpos - j * tk
            future = [2.0 * jnp.maximum((lane + sb * LANES) - qrel, 0).astype(F32) for sb in range(nslab)]

        for c in range(A_KV_HEADS):
            ps, alphas = [], []
            for g in range(A_GROUP):
                slope = slopes2[c * A_GROUP + g]
                rows = slice(g * tq, (g + 1) * tq)
                ts = [ss[c][rows, sb * LANES:(sb + 1) * LANES] for sb in range(nslab)]
                if diag:
                    ts = [t - slope * f for t, f in zip(ts, future)]
                c_j = slope * base
                mx = ts[0]
                for t in ts[1:]:
                    mx = jnp.maximum(mx, t)
                m_old = m_ref[c, rows, :]
                m_new = jnp.maximum(m_old, jnp.max(mx, axis=1, keepdims=True) + c_j)
                ref = m_new - c_j
                alphas.append(jnp.exp2(m_old - m_new))
                ps.append(jnp.concatenate([jnp.exp2(t - ref).astype(BF16) for t in ts], axis=1))
                m_ref[c, rows, :] = m_new
            pv = jnp.dot(jnp.concatenate(ps, axis=0), vvs[c], preferred_element_type=F32)
            acc_ref[c] = acc_ref[c] * jnp.concatenate(alphas, axis=0) + pv

    def visible_chunk(j, carry):
        process_chunk(j, diag=False)
        return carry

    lax.fori_loop(0, nk - 1, visible_chunk, 0)
    process_chunk(nk - 1, diag=True)

    for c in range(A_KV_HEADS):
        for g in range(A_GROUP):
            h = c * A_GROUP + g
            a = acc_ref[c, g * tq:(g + 1) * tq, :]
            val = a / pltpu.roll(a, HALF, axis=1)
            off = (c % 2) * HALF
            o_ref[0, :, h * A_HEAD_DIM:(h + 1) * A_HEAD_DIM] = val[:, off:off + HALF].astype(o_ref.dtype)


def _attend_a(q, qi, kiwi_q, kiwi_k, k, v, *, tk, pos0, l_valid, n_sel):
    b, t, _ = q.shape
    tq = LANES
    t_pad = -(-t // tq) * tq
    if t_pad != t:
        q, qi, kiwi_q = (jnp.pad(a, ((0, 0), (0, t_pad - t), (0, 0))) for a in (q, qi, kiwi_q))
    l = k.shape[1]
    nkc = l // tk
    assert tk % tq == 0 and pos0 % tq == 0 and l % tk == 0 and tk <= 32 * 256 and pos0 + t_pad <= l
    pos_bits = max(1, int(math.ceil(math.log2(l + 1))))
    kvc = A_KV_HEADS * A_HEAD_DIM
    kiwi_k = kiwi_k.reshape(b, nkc, tk, LANES)
    k = k.reshape(b, nkc, tk, kvc)
    v = v.reshape(b, nkc, tk, kvc)
    kern = functools.partial(_attn_a_kernel, tq=tq, tk=tk, pos0=pos0, l_valid=l_valid, t_valid=t,
                             n_sel=n_sel, pos_bits=pos_bits)
    ocols = A_HEADS * A_HEAD_DIM
    out = pl.pallas_call(
        kern,
        grid=(b, t_pad // tq),
        in_specs=[
            pl.BlockSpec((1, tq, A_HEADS * LANES), lambda bi, i: (bi, i, 0)),
            pl.BlockSpec((1, tq, IDX_HEADS * LANES), lambda bi, i: (bi, i, 0)),
            pl.BlockSpec((1, tq, LANES), lambda bi, i: (bi, i, 0)),
            pl.BlockSpec((1, nkc, tk, LANES), lambda bi, i: (bi, 0, 0, 0)),
            pl.BlockSpec((1, nkc, tk, kvc), lambda bi, i: (bi, 0, 0, 0)),
            pl.BlockSpec((1, nkc, tk, kvc), lambda bi, i: (bi, 0, 0, 0)),
        ],
        out_specs=pl.BlockSpec((1, tq, ocols), lambda bi, i: (bi, i, 0)),
        out_shape=jax.ShapeDtypeStruct((b, t_pad, ocols), BF16),
        scratch_shapes=[
            pltpu.VMEM((nkc, tk, tq), I32),
            pltpu.VMEM((2, tk, LANES), BF16),
            pltpu.VMEM((A_KV_HEADS, A_GROUP * tq, 2 * LANES), BF16),
            pltpu.VMEM((IDX_HEADS * tq, LANES), BF16),
            pltpu.VMEM((A_KV_HEADS, A_GROUP * tq, LANES), F32),
            pltpu.VMEM((A_KV_HEADS, A_GROUP * tq, LANES), F32),
        ],
        compiler_params=_compiler_params(("parallel", "arbitrary")),
        name="attn_a",
    )(q, qi, kiwi_q, kiwi_k, k, v)
    return out[:, :t]


def _attn_b_kernel(q_ref, k_ref, v_ref, lam_ref, slope_ref, g_ref, o_ref,
                   qq_ref, m_ref, acc_ref, *, tq, tk, pos0, l_valid, lambda_init):
    h = pl.program_id(1)
    i = pl.program_id(2)
    nslab = tk // LANES
    qpos, limit, nk = _block_geometry(pos0, i, tq, tk, l_valid)
    slope2 = slope_ref[pl.ds(h, 1), :] * LOG2E
    lane = lax.broadcasted_iota(I32, (1, LANES), 1)

    q = q_ref[0]
    lane_q = lax.broadcasted_iota(I32, (tq, LANES), 1)
    zero = jnp.zeros_like(q)
    qq_ref[0:tq, :] = jnp.where(lane_q < B_HEAD_DIM, q, zero)
    qq_ref[tq:2 * tq, :] = jnp.where(lane_q >= B_HEAD_DIM, q, zero)

    m_ref[...] = jnp.full(m_ref.shape, M_INIT, F32)
    acc_ref[...] = jnp.zeros(acc_ref.shape, F32)

    def finish_rows(j, rs, ts, vv):
        c_j = slope2 * (j * tk).astype(F32)
        mx = ts[0]
        for t in ts[1:]:
            mx = jnp.maximum(mx, t)
        m_old = m_ref[rs, :]
        m_new = jnp.maximum(m_old, jnp.max(mx, axis=1, keepdims=True) + c_j)
        ref = m_new - c_j
        alpha = jnp.exp2(m_old - m_new)
        p = jnp.concatenate([jnp.exp2(t - ref).astype(BF16) for t in ts], axis=1)
        pv = jnp.dot(p, vv, preferred_element_type=F32)
        acc_ref[rs, :] = acc_ref[rs, :] * jnp.concatenate([alpha, alpha], axis=1) + pv
        m_ref[rs, :] = m_new

    def chunk_operands(j):
        vc = v_ref[0, j]
        return k_ref[0, j], jnp.concatenate([vc, jnp.ones_like(vc)], axis=1)

    rg = min(tq, B_ROW_GROUP)
    groups = [slice(r, r + rg) for r in range(0, 2 * tq, rg)]

    def visible_chunk(j, carry):
        kc, vv = chunk_operands(j)
        ss = [lax.dot_general(qq_ref[rs, :], kc, NT_DIMS, preferred_element_type=F32) for rs in groups]
        for rs, s in zip(groups, ss):
            ts = []
            for g in range(nslab):
                kb = slope2 * (lane + g * LANES).astype(F32)
                ts.append(s[:, g * LANES:(g + 1) * LANES] + kb)
            finish_rows(j, rs, ts, vv)
        return carry

    lax.fori_loop(0, nk - 1, visible_chunk, 0)

    j = nk - 1
    kc, vv = chunk_operands(j)
    qrel = qpos - j * tk
    qrel_f = qrel.astype(F32)
    lim_rel = limit - j * tk
    biases, oks = [], []
    for g in range(nslab):
        krel = lane + g * LANES
        biases.append(slope2 * (jnp.abs((qrel - krel).astype(F32)) - qrel_f))
        oks.append(krel < lim_rel)
    ss = [lax.dot_general(qq_ref[rs, :], kc, NT_DIMS, preferred_element_type=F32) for rs in groups]
    for rs, s in zip(groups, ss):
        qr = slice(rs.start % tq, rs.start % tq + rg)
        ts = [jnp.where(oks[g][qr], s[:, g * LANES:(g + 1) * LANES] - biases[g][qr], NEG_FILL)
              for g in range(nslab)]
        finish_rows(j, rs, ts, vv)

    hd = 2 * B_HEAD_DIM
    lp = lam_ref[...]
    lam = (jnp.exp(jnp.sum(lp[0:1] * lp[1:2], axis=1, keepdims=True))
           - jnp.exp(jnp.sum(lp[2:3] * lp[3:4], axis=1, keepdims=True)) + lambda_init)
    out = (acc_ref[0:tq, 0:hd] / acc_ref[0:tq, hd:2 * hd]
           - lam * (acc_ref[tq:2 * tq, 0:hd] / acc_ref[tq:2 * tq, hd:2 * hd]))
    rms = lax.rsqrt(jnp.mean(out * out, axis=1, keepdims=True) + LN_EPS)
    o_ref[0] = (out * rms * g_ref[...] * (1.0 - lambda_init)).astype(o_ref.dtype)


def _attend_b(q, k, v, lam_params, subln_g, *, tq, tk, pos0, l_valid, lambda_init):
    b, t, cols = q.shape
    l = k.shape[1]
    nkc = l // tk
    hd = 2 * B_HEAD_DIM
    assert tk % tq == 0 and pos0 % tq == 0 and t % tq == 0 and l % tk == 0
    k = k.reshape(b, nkc, tk, cols)
    v = v.reshape(b, nkc, tk, cols)
    slopes = jnp.broadcast_to(jnp.asarray(_alibi_slopes(B_HEADS), F32)[:, None], (B_HEADS, LANES))
    kern = functools.partial(_attn_b_kernel, tq=tq, tk=tk, pos0=pos0, l_valid=l_valid,
                             lambda_init=lambda_init)
    return pl.pallas_call(
        kern,
        grid=(b, B_HEADS, t // tq),
        in_specs=[
            pl.BlockSpec((1, tq, hd), lambda bi, h, i: (bi, i, h)),
            pl.BlockSpec((1, nkc, tk, hd), lambda bi, h, i: (bi, 0, 0, h)),
            pl.BlockSpec((1, nkc, tk, hd), lambda bi, h, i: (bi, 0, 0, h)),
            pl.BlockSpec((4, B_HEAD_DIM), lambda bi, h, i: (0, 0)),
            pl.BlockSpec((B_HEADS, LANES), lambda bi, h, i: (0, 0)),
            pl.BlockSpec((1, hd), lambda bi, h, i: (0, 0)),
        ],
        out_specs=pl.BlockSpec((1, tq, hd), lambda bi, h, i: (bi, i, h)),
        out_shape=jax.ShapeDtypeStruct((b, t, cols), BF16),
        scratch_shapes=[
            pltpu.VMEM((2 * tq, hd), BF16),
            pltpu.VMEM((2 * tq, LANES), F32),
            pltpu.VMEM((2 * tq, 2 * hd), F32),
        ],
        compiler_params=_compiler_params(("parallel", "parallel", "arbitrary")),
        name="attn_b",
    )(q, k, v, lam_params, slopes, subln_g.reshape(1, hd))


def _layer_norm(z, g, b):
    mu = jnp.mean(z, axis=1, keepdims=True)
    zc = z - mu
    var = jnp.mean(zc * zc, axis=1, keepdims=True)
    return zc * lax.rsqrt(var + LN_EPS) * g + b


def _post_kernel(x_ref, a_ref, w_ref, g_ref, b_ref, o_ref):
    mix = jnp.dot(a_ref[...], w_ref[...], preferred_element_type=F32)
    o_ref[...] = _layer_norm(ALPHA * x_ref[...] + mix, g_ref[...], b_ref[...])


def _post_mixer(x, a, w_out, g, b, tm=512):
    m, d = x.shape
    tm = _row_tile(m, tm)
    return pl.pallas_call(
        _post_kernel,
        grid=(m // tm,),
        in_specs=[
            pl.BlockSpec((tm, d), lambda i: (i, 0)),
            pl.BlockSpec((tm, a.shape[1]), lambda i: (i, 0)),
            pl.BlockSpec(w_out.shape, lambda i: (0, 0)),
            pl.BlockSpec((1, d), lambda i: (0, 0)),
            pl.BlockSpec((1, d), lambda i: (0, 0)),
        ],
        out_specs=pl.BlockSpec((tm, d), lambda i: (i, 0)),
        out_shape=jax.ShapeDtypeStruct((m, d), F32),
        compiler_params=_compiler_params(("parallel",)),
        name="post_mixer",
    )(x, a, w_out, g.reshape(1, d), b.reshape(1, d))


def _sigmoid(x):
    return 1.0 / (1.0 + jnp.exp(-x))


def _ffn_kernel(x_ref, p_ref, w1_ref, w3_ref, w2_ref, wg_ref, wp_ref, g_ref, b_ref, o_ref, acc_ref):
    f = pl.program_id(1)
    x = x_ref[...]
    xb = x.astype(BF16)

    @pl.when(f == 0)
    def _():
        gate = _sigmoid(jnp.dot(xb, wg_ref[...], preferred_element_type=F32))
        ple = jnp.dot(p_ref[...].astype(BF16), wp_ref[...], preferred_element_type=F32)
        acc_ref[...] = ALPHA * x + gate * ple

    gt = jnp.dot(xb, w1_ref[...], preferred_element_type=F32)
    up = jnp.dot(xb, w3_ref[...], preferred_element_type=F32)
    hid = (gt * _sigmoid(gt)) * up
    acc_ref[...] += jnp.dot(hid.astype(BF16), w2_ref[...], preferred_element_type=F32)

    @pl.when(f == pl.num_programs(1) - 1)
    def _():
        o_ref[...] = _layer_norm(acc_ref[...], g_ref[...], b_ref[...])


def _ffn_tile(d_ff):
    for nf in (1, 2, 3, 4, 5, 6, 7, 8, 11, 22):
        if d_ff % nf == 0 and (d_ff // nf) % LANES == 0 and d_ff // nf <= 1536:
            return d_ff // nf
    return d_ff


def _channel_and_ple(x, p, w1, w3, w2, wg, wp, g, b, tm=512):
    m, d = x.shape
    tm = _row_tile(m, tm)
    d_ff = w1.shape[1]
    tf = _ffn_tile(d_ff)
    return pl.pallas_call(
        _ffn_kernel,
        grid=(m // tm, d_ff // tf),
        in_specs=[
            pl.BlockSpec((tm, d), lambda i, f: (i, 0)),
            pl.BlockSpec((tm, p.shape[1]), lambda i, f: (i, 0)),
            pl.BlockSpec((d, tf), lambda i, f: (0, f)),
            pl.BlockSpec((d, tf), lambda i, f: (0, f)),
            pl.BlockSpec((tf, d), lambda i, f: (f, 0)),
            pl.BlockSpec(wg.shape, lambda i, f: (0, 0)),
            pl.BlockSpec(wp.shape, lambda i, f: (0, 0)),
            pl.BlockSpec((1, d), lambda i, f: (0, 0)),
            pl.BlockSpec((1, d), lambda i, f: (0, 0)),
        ],
        out_specs=pl.BlockSpec((tm, d), lambda i, f: (i, 0)),
        out_shape=jax.ShapeDtypeStruct((m, d), F32),
        scratch_shapes=[pltpu.VMEM((tm, d), F32)],
        compiler_params=_compiler_params(("parallel", "arbitrary")),
        name="ffn_ple",
    )(x, p, w1, w3, w2, wg, wp, g.reshape(1, d), b.reshape(1, d))


def _pad_keys(x, mult):
    l = x.shape[1]
    lp = -(-l // mult) * mult
    if lp == l:
        return x
    return jnp.pad(x, ((0, 0), (0, lp - l)) + ((0, 0),) * (x.ndim - 2))


def _head_slabs(w, n_heads, dim, high_half):
    k = w.shape[0]
    w3 = w.reshape(k, n_heads, dim)
    low = jnp.pad(w3, ((0, 0), (0, 0), (0, LANES - dim)))
    high = jnp.pad(w3, ((0, 0), (0, 0), (LANES - dim, 0)))
    sel = jnp.asarray(high_half, bool)[None, :, None]
    return jnp.where(sel, high, low).reshape(k, n_heads * LANES)


def _split_a_weights(w_in):
    qc = A_HEADS * A_HEAD_DIM
    kvc = A_KV_HEADS * A_HEAD_DIM
    ic = IDX_HEADS * IDX_DIM
    w_q, w_k, w_v = w_in[:, :qc], w_in[:, qc:qc + kvc], w_in[:, qc + kvc:qc + 2 * kvc]
    o = qc + 2 * kvc
    w_qi = w_in[:, o:o + ic]
    tail = w_in[:, o + ic:o + ic + IDX_DIM + IDX_HEADS]
    tail = jnp.pad(tail, ((0, 0), (0, LANES - tail.shape[1])))
    q_high = [(h // A_GROUP) % 2 == 1 for h in range(A_HEADS)]
    pieces = [_head_slabs(w_q, A_HEADS, A_HEAD_DIM, q_high), w_k, w_v,
              _head_slabs(w_qi, IDX_HEADS, IDX_DIM, [False] * IDX_HEADS), tail]
    return [p.astype(BF16) for p in pieces]


def _project_a(x2d, w_pieces):
    plan = [[(BF16, A_HEAD_DIM ** -0.5 * LOG2E)],
            [(F32, 1.0), (BF16, 1.0)],
            [(F32, 1.0), (BF16, 1.0)],
            [(BF16, 1.0)],
            [(F32, 1.0), (BF16, 1.0)]]
    return _project(x2d, w_pieces, plan)


def _mixer_a(xp, xs, cache_k, cache_v, cache_ki, w_in):
    bp, s, d = xp.shape
    bs, t, _ = xs.shape
    past = cache_k.shape[1]
    kvc = A_KV_HEADS * A_HEAD_DIM
    w_pieces = _split_a_weights(w_in)

    q, k32, k16, v32, v16, qi, kiwi32, kiwi16 = _project_a(xp.reshape(bp * s, d), w_pieces)
    n_sel_p = min(TOPK_MAX, s // 4)
    tk_p = 512 if s % 512 == 0 else s
    att_p = _attend_a(q.reshape(bp, s, -1), qi.reshape(bp, s, -1), kiwi32.reshape(bp, s, LANES),
                      kiwi16.reshape(bp, s, LANES), k16.reshape(bp, s, kvc), v16.reshape(bp, s, kvc),
                      tk=tk_p, pos0=0, l_valid=s, n_sel=n_sel_p)
    new_p = (k32.reshape(bp, s, A_KV_HEADS, A_HEAD_DIM), v32.reshape(bp, s, A_KV_HEADS, A_HEAD_DIM),
             kiwi32.reshape(bp, s, LANES)[:, :, :IDX_DIM])

    qs, k32s, k16s, v32s, v16s, qis, kiwi32s, kiwi16s = _project_a(xs.reshape(bs * t, d), w_pieces)
    l_s = past + t
    n_sel_s = min(TOPK_MAX, l_s // 4)
    tk_s = 384
    k_all = _pad_keys(jnp.concatenate([cache_k.reshape(bs, past, kvc).astype(BF16),
                                       k16s.reshape(bs, t, kvc)], axis=1), tk_s)
    v_all = _pad_keys(jnp.concatenate([cache_v.reshape(bs, past, kvc).astype(BF16),
                                       v16s.reshape(bs, t, kvc)], axis=1), tk_s)
    ki_cache = jnp.pad(cache_ki.astype(BF16), ((0, 0), (0, 0), (0, LANES - IDX_DIM)))
    ki_all = _pad_keys(jnp.concatenate([ki_cache, kiwi16s.reshape(bs, t, LANES)], axis=1), tk_s)
    att_s = _attend_a(qs.reshape(bs, t, -1), qis.reshape(bs, t, -1), kiwi32s.reshape(bs, t, LANES),
                      ki_all, k_all, v_all, tk=tk_s, pos0=past, l_valid=l_s, n_sel=n_sel_s)
    new_s = (k32s.reshape(bs, t, A_KV_HEADS, A_HEAD_DIM), v32s.reshape(bs, t, A_KV_HEADS, A_HEAD_DIM),
             kiwi32s.reshape(bs, t, LANES)[:, :, :IDX_DIM])
    return att_p.reshape(bp * s, -1), att_s.reshape(bs * t, -1), new_p, new_s


def _project_b(x2d, w_in_bf):
    cols = B_HEADS * 2 * B_HEAD_DIM
    w_pieces = [w_in_bf[:, :cols], w_in_bf[:, cols:2 * cols], w_in_bf[:, 2 * cols:]]
    plan = [[(BF16, B_HEAD_DIM ** -0.5 * LOG2E)],
            [(F32, 1.0), (BF16, 1.0)],
            [(F32, 1.0), (BF16, 1.0)]]
    return _project(x2d, w_pieces, plan)


def _mixer_b(xp, xs, cache_k, cache_v, w_in, lam_params, subln_g, lambda_init):
    bp, s, d = xp.shape
    bs, t, _ = xs.shape
    past = cache_k.shape[1]
    cols = B_HEADS * 2 * B_HEAD_DIM
    w_in_bf = w_in.astype(BF16)

    q, k32, k16, v32, v16 = _project_b(xp.reshape(bp * s, d), w_in_bf)
    tq_p = 512 if s % 512 == 0 else s
    tk_p = 1024 if s % 1024 == 0 else s
    att_p = _attend_b(q.reshape(bp, s, cols), k16.reshape(bp, s, cols), v16.reshape(bp, s, cols),
                      lam_params, subln_g, tq=tq_p, tk=tk_p, pos0=0, l_valid=s, lambda_init=lambda_init)
    new_p = (k32.reshape(bp, s, B_HEADS, 2 * B_HEAD_DIM), v32.reshape(bp, s, B_HEADS, 2 * B_HEAD_DIM))

    qs, k32s, k16s, v32s, v16s = _project_b(xs.reshape(bs * t, d), w_in_bf)
    l_s = past + t
    tk_s = 384
    k_all = _pad_keys(jnp.concatenate([cache_k.reshape(bs, past, cols).astype(BF16),
                                       k16s.reshape(bs, t, cols)], axis=1), tk_s)
    v_all = _pad_keys(jnp.concatenate([cache_v.reshape(bs, past, cols).astype(BF16),
                                       v16s.reshape(bs, t, cols)], axis=1), tk_s)
    att_s = _attend_b(qs.reshape(bs, t, cols), k_all, v_all, lam_params, subln_g,
                      tq=t, tk=tk_s, pos0=past, l_valid=l_s, lambda_init=lambda_init)
    new_s = (k32s.reshape(bs, t, B_HEADS, 2 * B_HEAD_DIM), v32s.reshape(bs, t, B_HEADS, 2 * B_HEAD_DIM))
    return att_p.reshape(bp * s, cols), att_s.reshape(bs * t, cols), new_p, new_s


def kernel(x_prompt, x_sample, cache_a_k, cache_a_v, cache_a_kidx, cache_b_k, cache_b_v, p_prompt, p_sample, a_w_in, a_w_out, b_w_in, b_lambda, b_subln, b_w_out, ffn_w13, ffn_w2, ple_w_proj, ple_w_gate, ln_gain, ln_bias):
    bp, s, d = x_prompt.shape
    bs, t, _ = x_sample.shape
    depth = ffn_w13.shape[0]
    d_ff = ffn_w2.shape[1]

    yp = x_prompt.reshape(bp * s, d)
    ys = x_sample.reshape(bs * t, d)
    a_new_p, a_new_s, b_new_p, b_new_s = [], [], [], []
    for i in range(depth):
        j = i // 2
        if i % 2 == 0:
            mp, ms, new_p, new_s = _mixer_a(yp.reshape(bp, s, d), ys.reshape(bs, t, d),
                                            cache_a_k[j], cache_a_v[j], cache_a_kidx[j], a_w_in[j])
            w_out = a_w_out[j].astype(BF16)
            a_new_p.append(new_p)
            a_new_s.append(new_s)
        else:
            lambda_init = 0.8 - 0.6 * math.exp(-0.3 * i)
            mp, ms, new_p, new_s = _mixer_b(yp.reshape(bp, s, d), ys.reshape(bs, t, d),
                                            cache_b_k[j], cache_b_v[j], b_w_in[j], b_lambda[j], b_subln[j],
                                            lambda_init)
            w_out = b_w_out[j].astype(BF16)
            b_new_p.append(new_p)
            b_new_s.append(new_s)
        yp = _post_mixer(yp, mp, w_out, ln_gain[i, 0], ln_bias[i, 0])
        ys = _post_mixer(ys, ms, w_out, ln_gain[i, 0], ln_bias[i, 0])
        w13 = ffn_w13[i].astype(BF16)
        w1, w3 = w13[:, :d_ff], w13[:, d_ff:]
        w2 = ffn_w2[i].astype(BF16)
        wg = ple_w_gate[i].astype(BF16)
        wp = ple_w_proj[i].astype(BF16)
        yp = _channel_and_ple(yp, p_prompt[i].reshape(bp * s, -1), w1, w3, w2, wg, wp, ln_gain[i, 1], ln_bias[i, 1])
        ys = _channel_and_ple(ys, p_sample[i].reshape(bs * t, -1), w1, w3, w2, wg, wp, ln_gain[i, 1], ln_bias[i, 1])

    def stack(items, idx):
        return jnp.stack([it[idx] for it in items])

    return (yp.reshape(bp, s, d), ys.reshape(bs, t, d),
            stack(a_new_p, 0), stack(a_new_p, 1), stack(a_new_p, 2),
            stack(b_new_p, 0), stack(b_new_p, 1),
            stack(a_new_s, 0), stack(a_new_s, 1), stack(a_new_s, 2),
            stack(b_new_s, 0), stack(b_new_s, 1))
```

```python
import functools
import math

import jax
import jax.numpy as jnp
import ml_dtypes
import numpy as np
from jax import lax
from jax.experimental import pallas as pl
from jax.experimental.pallas import tpu as pltpu

F32 = jnp.float32
BF16 = jnp.bfloat16
I32 = jnp.int32

CHUNK = 64
CHUNK_SHIFT = 6
A_HEADS = 16
A_HEAD_DIM = 64
A_KV_HEADS = 4
A_GROUP = A_HEADS // A_KV_HEADS
IDX_HEADS = 8
IDX_DIM = 64
TOPK_MAX = 256
IDX_SCALE = IDX_DIM ** -0.5 * IDX_HEADS ** -0.5
B_HEADS = 8
B_HEAD_DIM = 64
DEPTH = 2
ALPHA = (2 * DEPTH) ** 0.25
LN_EPS = 1e-5

LANES = 128
HALF = LANES // 2
VMEM_LIMIT_BYTES = 56 * 1024 * 1024

NEG_FILL = -1e30
M_INIT = -5e29
INT_MIN = -(2 ** 31)
KEY_NEG_INF = int(np.int32(np.uint32(0xFF800000) ^ np.uint32(0x7FFFFFFF)))
POS_BIG = 2 ** 30

B_ROW_GROUP = 256
COUNT_ROWS = 64
BITS_PER_CHECK = 4
LOG2E = float(np.log2(np.e))
NT_DIMS = (((1,), (1,)), ((), ()))


def _alibi_slopes(n_heads):
    return [float(2.0 ** (-8.0 * (h + 1) / n_heads)) for h in range(n_heads)]


def _compiler_params(semantics):
    return pltpu.CompilerParams(dimension_semantics=semantics,
                                vmem_limit_bytes=VMEM_LIMIT_BYTES)


def _row_tile(m, want):
    return want if m % want == 0 else m


def _block_geometry(pos0, i, tq, tk, l_valid):
    q0 = pos0 + i * tq
    qpos = q0 + lax.broadcasted_iota(I32, (tq, 1), 0)
    limit = jnp.minimum((lax.shift_right_logical(qpos, CHUNK_SHIFT) + 1) * CHUNK, l_valid)
    lim_max = jnp.minimum((lax.shift_right_logical(q0 + tq - 1, CHUNK_SHIFT) + 1) * CHUNK, l_valid)
    return qpos, limit, (lim_max + tk - 1) // tk


def _proj_kernel(x_ref, *refs, out_plan):
    n_w = len(out_plan)
    w_refs, o_refs = refs[:n_w], refs[n_w:]
    x = x_ref[...].astype(BF16)
    oi = 0
    for w_ref, outs in zip(w_refs, out_plan):
        y = jnp.dot(x, w_ref[...], preferred_element_type=F32)
        for dtype, scale in outs:
            o_refs[oi][...] = (y if scale == 1.0 else y * scale).astype(dtype)
            oi += 1


def _project(x, weights, out_plan, tm=512):
    m, k = x.shape
    tm = _row_tile(m, tm)
    in_specs = [pl.BlockSpec((tm, k), lambda i: (i, 0))]
    out_shape, out_specs = [], []
    for w, outs in zip(weights, out_plan):
        n = w.shape[1]
        in_specs.append(pl.BlockSpec((k, n), lambda i: (0, 0)))
        for dtype, _ in outs:
            out_shape.append(jax.ShapeDtypeStruct((m, n), dtype))
            out_specs.append(pl.BlockSpec((tm, n), lambda i: (i, 0)))
    return pl.pallas_call(
        functools.partial(_proj_kernel, out_plan=out_plan),
        grid=(m // tm,),
        in_specs=in_specs,
        out_specs=out_specs,
        out_shape=out_shape,
        compiler_params=_compiler_params(("parallel",)),
        name="proj",
    )(x, *weights)


def _row_fold(x, rows):
    part = x[0:rows]
    for s in range(1, x.shape[0] // rows):
        part = part + x[s * rows:(s + 1) * rows]
    return part


def _split_bf16(x):
    hi = float(np.float32(x).astype(ml_dtypes.bfloat16))
    return hi, x - hi


def _attn_a_kernel(q_ref, qi_ref, wq_ref, kiwi_ref, k_ref, v_ref, o_ref,
                   key_ref, pf_ref, qs_ref, qis_ref, m_ref, acc_ref,
                   *, tq, tk, pos0, l_valid, t_valid, n_sel, pos_bits):
    i = pl.program_id(1)
    nslab = tk // LANES
    qpos, limit, nk = _block_geometry(pos0, i, tq, tk, l_valid)
    lane = lax.broadcasted_iota(I32, (1, LANES), 1)
    qpos_l = pos0 + i * tq + lax.broadcasted_iota(I32, (1, tq), 1)
    limit_l = jnp.minimum((lax.shift_right_logical(qpos_l, CHUNK_SHIFT) + 1) * CHUNK, l_valid)
    row_ok = lax.broadcasted_iota(I32, (1, tq), 1) < t_valid
    slopes2 = [s * LOG2E for s in _alibi_slopes(A_HEADS)]

    eye = jnp.where(lax.broadcasted_iota(I32, (tq, LANES), 0) == lax.broadcasted_iota(I32, (tq, LANES), 1),
                    1.0, 0.0).astype(BF16)
    lane_q = lax.broadcasted_iota(I32, (tq, LANES), 1)
    for c in range(A_KV_HEADS):
        fb = (1 - c % 2) * HALF
        for g in range(A_GROUP):
            h = c * A_GROUP + g
            s_hi, s_lo = _split_bf16(slopes2[h])
            coef = jnp.where(lane_q == fb, 32.0 * s_hi,
                             jnp.where(lane_q == fb + 1, s_hi,
                                       jnp.where(lane_q == fb + 2, 32.0 * s_lo, s_lo))).astype(BF16)
            is_coef = (lane_q >= fb) & (lane_q < fb + 4)
            rows = slice(g * tq, (g + 1) * tq)
            qs_ref[c, rows, 0:LANES] = jnp.where(is_coef, coef, q_ref[0, :, h * LANES:(h + 1) * LANES])
            qs_ref[c, rows, LANES:2 * LANES] = eye
    for h in range(IDX_HEADS):
        qis_ref[h * tq:(h + 1) * tq, :] = qi_ref[0, :, h * LANES:(h + 1) * LANES]
    krow = lax.broadcasted_iota(I32, (tk, LANES), 0)
    r_hi = lax.shift_right_logical(krow, 5).astype(F32)
    r_lo = (krow & 31).astype(F32)
    lane_k = lax.broadcasted_iota(I32, (tk, LANES), 1)
    for par in range(2):
        fb = (1 - par) * HALF
        pf_ref[par] = jnp.where((lane_k == fb) | (lane_k == fb + 2), r_hi,
                                jnp.where((lane_k == fb + 1) | (lane_k == fb + 3), r_lo, 0.0)).astype(BF16)

    w_t = jnp.transpose(wq_ref[0])[IDX_DIM:IDX_DIM + IDX_HEADS] * IDX_SCALE

    half_rows = IDX_HEADS // 2 * tq

    def score_body(j, carry):
        kic = kiwi_ref[0, j]
        ds = [lax.dot_general(kic, qis_ref[r * half_rows:(r + 1) * half_rows, :], NT_DIMS,
                              preferred_element_type=F32) for r in range(2)]
        sc = None
        for h in range(IDX_HEADS):
            d = ds[h // 4][:, (h % 4) * tq:(h % 4 + 1) * tq]
            term = jnp.maximum(d, 0.0) * w_t[h:h + 1]
            sc = term if sc is None else sc + term
        kpos = j * tk + lax.broadcasted_iota(I32, (tk, tq), 0)
        sc = jnp.where(kpos < limit_l, sc, -jnp.inf)
        bits = lax.bitcast_convert_type(sc, I32)
        key = bits ^ (lax.shift_right_arithmetic(bits, 31) & 0x7FFFFFFF)
        key_ref[j] = jnp.where(bits == INT_MIN, 0, key)
        return carry

    lax.fori_loop(0, nk, score_body, 0)

    def count(pred):
        def body(j, acc):
            kpos = j * tk + lax.broadcasted_iota(I32, (tk, tq), 0)
            return acc + _row_fold(jnp.where(pred(key_ref[j], kpos), 1.0, 0.0), COUNT_ROWS)
        acc = lax.fori_loop(0, nk, body, jnp.zeros((COUNT_ROWS, tq), F32))
        return jnp.sum(acc, axis=0, keepdims=True)

    k_f = float(n_sel)
    n_all = (jnp.zeros((1, tq), I32) + nk * tk).astype(F32)
    c_nonneg = count(lambda u, kp: u >= 0)
    take = c_nonneg >= k_f
    thr0 = jnp.where(take, 0, INT_MIN).astype(I32)
    c_ge0 = jnp.where(take, c_nonneg, n_all)

    def unsettled(c_ge):
        return jnp.sum(jnp.where((c_ge != k_f) & row_ok, 1.0, 0.0)) > 0.0

    def bits_cond(carry):
        grp, _, c_ge = carry
        return (grp * BITS_PER_CHECK < 31) & unsettled(c_ge)

    def bits_body(carry):
        grp, thr, c_ge = carry
        for bb in range(BITS_PER_CHECK):
            b = grp * BITS_PER_CHECK + bb
            bit = jnp.where(b <= 30, lax.shift_left(jnp.int32(1), 30 - jnp.minimum(b, 30)), 0)
            cand = thr | bit
            c = count(lambda u, kp: u >= cand)
            take = c >= k_f
            thr, c_ge = jnp.where(take, cand, thr), jnp.where(take, c, c_ge)
        return grp + 1, thr, c_ge

    _, thr, c_ge = lax.while_loop(bits_cond, bits_body, (jnp.int32(0), thr0, c_ge0))

    surplus = jnp.where((c_ge > k_f) & (thr != KEY_NEG_INF) & row_ok, 1.0, 0.0)
    any_surplus = jnp.sum(surplus) > 0.0

    def tie_cut():
        need = k_f - count(lambda u, kp: u > thr)

        def body(b, x):
            cand = x | lax.shift_left(jnp.int32(1), pos_bits - 1 - b)
            c = count(lambda u, kp: (u == thr) & (kp < cand))
            return jnp.where(c < need, cand, x)
        return lax.fori_loop(0, pos_bits, body, jnp.zeros((1, tq), I32))

    cut = lax.cond(any_surplus, tie_cut, lambda: jnp.full((1, tq), POS_BIG, I32))

    m_ref[...] = jnp.full(m_ref.shape, M_INIT, F32)
    acc_ref[...] = jnp.zeros(acc_ref.shape, F32)

    def process_chunk(j, diag):
        u = key_ref[j]
        kpos = j * tk + lax.broadcasted_iota(I32, (tk, tq), 0)
        sel = (u > thr) | ((u == thr) & (kpos <= cut))
        if diag:
            sel = sel & (kpos < limit_l)
        mask_t = jnp.where(sel, 0.0, NEG_FILL).astype(BF16)
        base = jnp.full((1, LANES), j * tk, I32).astype(F32)

        k_all = k_ref[0, j]
        v_all = v_ref[0, j]
        rhs, vvs = [], []
        for c in range(A_KV_HEADS):
            sl = slice((c // 2) * LANES, (c // 2 + 1) * LANES)
            keep = (lane_k < HALF) if c % 2 == 0 else (lane_k >= HALF)
            rhs.append(jnp.concatenate([jnp.where(keep, k_all[:, sl], pf_ref[c % 2]), mask_t], axis=1))
            vg = v_all[:, sl]
            vvs.append(jnp.where(keep, vg, jnp.ones_like(vg)))
        ss = [lax.dot_general(qs_ref[c], rhs[c], NT_DIMS, preferred_element_type=F32)
              for c in range(A_KV_HEADS)]

        if diag:
            qrel = qpos - j * tk
            future = [2.0 * jnp.maximum((lane + sb * LANES) - qrel, 0).astype(F32) for sb in range(nslab)]

        for c in range(A_KV_HEADS):
            ps, alphas = [], []
            for g in range(A_GROUP):
                slope = slopes2[c * A_GROUP + g]
                rows = slice(g * tq, (g + 1) * tq)
                ts = [ss[c][rows, sb * LANES:(sb + 1) * LANES] for sb in range(nslab)]
                if diag:
                    ts = [t - slope * f for t, f in zip(ts, future)]
                c_j = slope * base
                mx = ts[0]
                for t in ts[1:]:
                    mx = jnp.maximum(mx, t)
                m_old = m_ref[c, rows, :]
                m_new = jnp.maximum(m_old, jnp.max(mx, axis=1, keepdims=True) + c_j)
                ref = m_new - c_j
                alphas.append(jnp.exp2(m_old - m_new))
                ps.append(jnp.concatenate([jnp.exp2(t - ref).astype(BF16) for t in ts], axis=1))
                m_ref[c, rows, :] = m_new
            pv = jnp.dot(jnp.concatenate(ps, axis=0), vvs[c], preferred_element_type=F32)
            acc_ref[c] = acc_ref[c] * jnp.concatenate(alphas, axis=0) + pv

    def visible_chunk(j, carry):
        process_chunk(j, diag=False)
        return carry

    lax.fori_loop(0, nk - 1, visible_chunk, 0)
    process_chunk(nk - 1, diag=True)

    for c in range(A_KV_HEADS):
        for g in range(A_GROUP):
            h = c * A_GROUP + g
            a = acc_ref[c, g * tq:(g + 1) * tq, :]
            val = a / pltpu.roll(a, HALF, axis=1)
            off = (c % 2) * HALF
            o_ref[0, :, h * A_HEAD_DIM:(h + 1) * A_HEAD_DIM] = val[:, off:off + HALF].astype(o_ref.dtype)


def _attend_a(q, qi, kiwi_q, kiwi_k, k, v, *, tk, pos0, l_valid, n_sel):
    b, t, _ = q.shape
    tq = LANES
    t_pad = -(-t // tq) * tq
    if t_pad != t:
        q, qi, kiwi_q = (jnp.pad(a, ((0, 0), (0, t_pad - t), (0, 0))) for a in (q, qi, kiwi_q))
    l = k.shape[1]
    nkc = l // tk
    assert tk % tq == 0 and pos0 % tq == 0 and l % tk == 0 and tk <= 32 * 256 and pos0 + t_pad <= l
    pos_bits = max(1, int(math.ceil(math.log2(l + 1))))
    kvc = A_KV_HEADS * A_HEAD_DIM
    kiwi_k = kiwi_k.reshape(b, nkc, tk, LANES)
    k = k.reshape(b, nkc, tk, kvc)
    v = v.reshape(b, nkc, tk, kvc)
    kern = functools.partial(_attn_a_kernel, tq=tq, tk=tk, pos0=pos0, l_valid=l_valid, t_valid=t,
                             n_sel=n_sel, pos_bits=pos_bits)
    ocols = A_HEADS * A_HEAD_DIM
    out = pl.pallas_call(
        kern,
        grid=(b, t_pad // tq),
        in_specs=[
            pl.BlockSpec((1, tq, A_HEADS * LANES), lambda bi, i: (bi, i, 0)),
            pl.BlockSpec((1, tq, IDX_HEADS * LANES), lambda bi, i: (bi, i, 0)),
            pl.BlockSpec((1, tq, LANES), lambda bi, i: (bi, i, 0)),
            pl.BlockSpec((1, nkc, tk, LANES), lambda bi, i: (bi, 0, 0, 0)),
            pl.BlockSpec((1, nkc, tk, kvc), lambda bi, i: (bi, 0, 0, 0)),
            pl.BlockSpec((1, nkc, tk, kvc), lambda bi, i: (bi, 0, 0, 0)),
        ],
        out_specs=pl.BlockSpec((1, tq, ocols), lambda bi, i: (bi, i, 0)),
        out_shape=jax.ShapeDtypeStruct((b, t_pad, ocols), BF16),
        scratch_shapes=[
            pltpu.VMEM((nkc, tk, tq), I32),
            pltpu.VMEM((2, tk, LANES), BF16),
            pltpu.VMEM((A_KV_HEADS, A_GROUP * tq, 2 * LANES), BF16),
            pltpu.VMEM((IDX_HEADS * tq, LANES), BF16),
            pltpu.VMEM((A_KV_HEADS, A_GROUP * tq, LANES), F32),
            pltpu.VMEM((A_KV_HEADS, A_GROUP * tq, LANES), F32),
        ],
        compiler_params=_compiler_params(("parallel", "arbitrary")),
        name="attn_a",
    )(q, qi, kiwi_q, kiwi_k, k, v)
    return out[:, :t]


def _attn_b_kernel(q_ref, k_ref, v_ref, lam_ref, slope_ref, g_ref, o_ref,
                   qq_ref, m_ref, acc_ref, *, tq, tk, pos0, l_valid, lambda_init):
    h = pl.program_id(1)
    i = pl.program_id(2)
    nslab = tk // LANES
    qpos, limit, nk = _block_geometry(pos0, i, tq, tk, l_valid)
    slope2 = slope_ref[pl.ds(h, 1), :] * LOG2E
    lane = lax.broadcasted_iota(I32, (1, LANES), 1)

    q = q_ref[0]
    lane_q = lax.broadcasted_iota(I32, (tq, LANES), 1)
    zero = jnp.zeros_like(q)
    qq_ref[0:tq, :] = jnp.where(lane_q < B_HEAD_DIM, q, zero)
    qq_ref[tq:2 * tq, :] = jnp.where(lane_q >= B_HEAD_DIM, q, zero)

    m_ref[...] = jnp.full(m_ref.shape, M_INIT, F32)
    acc_ref[...] = jnp.zeros(acc_ref.shape, F32)

    def finish_rows(j, rs, ts, vv):
        c_j = slope2 * (j * tk).astype(F32)
        mx = ts[0]
        for t in ts[1:]:
            mx = jnp.maximum(mx, t)
        m_old = m_ref[rs, :]
        m_new = jnp.maximum(m_old, jnp.max(mx, axis=1, keepdims=True) + c_j)
        ref = m_new - c_j
        alpha = jnp.exp2(m_old - m_new)
        p = jnp.concatenate([jnp.exp2(t - ref).astype(BF16) for t in ts], axis=1)
        pv = jnp.dot(p, vv, preferred_element_type=F32)
        acc_ref[rs, :] = acc_ref[rs, :] * jnp.concatenate([alpha, alpha], axis=1) + pv
        m_ref[rs, :] = m_new

    def chunk_operands(j):
        vc = v_ref[0, j]
        return k_ref[0, j], jnp.concatenate([vc, jnp.ones_like(vc)], axis=1)

    rg = min(tq, B_ROW_GROUP)
    groups = [slice(r, r + rg) for r in range(0, 2 * tq, rg)]

    def visible_chunk(j, carry):
        kc, vv = chunk_operands(j)
        ss = [lax.dot_general(qq_ref[rs, :], kc, NT_DIMS, preferred_element_type=F32) for rs in groups]
        for rs, s in zip(groups, ss):
            ts = []
            for g in range(nslab):
                kb = slope2 * (lane + g * LANES).astype(F32)
                ts.append(s[:, g * LANES:(g + 1) * LANES] + kb)
            finish_rows(j, rs, ts, vv)
        return carry

    lax.fori_loop(0, nk - 1, visible_chunk, 0)

    j = nk - 1
    kc, vv = chunk_operands(j)
    qrel = qpos - j * tk
    qrel_f = qrel.astype(F32)
    lim_rel = limit - j * tk
    block_starts_chunk = tq == tk and pos0 % tk == 0
    plans = []
    for rs in groups:
        q_lo = rs.start % tq
        n_vis = (q_lo + rg) // LANES if block_starts_chunk else nslab
        n_past = q_lo // LANES if block_starts_chunk else 0
        plans.append((rs, slice(q_lo, q_lo + rg), n_vis, n_past))
    ss = [lax.dot_general(qq_ref[rs, :], kc[0:n_vis * LANES], NT_DIMS, preferred_element_type=F32)
          for rs, _, n_vis, _ in plans]
    for (rs, qr, n_vis, n_past), s in zip(plans, ss):
        ts = []
        for g in range(n_vis):
            krel = lane + g * LANES
            sg = s[:, g * LANES:(g + 1) * LANES]
            if g < n_past:
                ts.append(sg + slope2 * krel.astype(F32))
            else:
                bias = slope2 * (jnp.abs((qrel[qr] - krel).astype(F32)) - qrel_f[qr])
                ts.append(jnp.where(krel < lim_rel[qr], sg - bias, NEG_FILL))
        finish_rows(j, rs, ts, vv[0:n_vis * LANES])

    hd = 2 * B_HEAD_DIM
    lp = lam_ref[...]
    lam = (jnp.exp(jnp.sum(lp[0:1] * lp[1:2], axis=1, keepdims=True))
           - jnp.exp(jnp.sum(lp[2:3] * lp[3:4], axis=1, keepdims=True)) + lambda_init)
    out = (acc_ref[0:tq, 0:hd] / acc_ref[0:tq, hd:2 * hd]
           - lam * (acc_ref[tq:2 * tq, 0:hd] / acc_ref[tq:2 * tq, hd:2 * hd]))
    rms = lax.rsqrt(jnp.mean(out * out, axis=1, keepdims=True) + LN_EPS)
    o_ref[0] = (out * rms * g_ref[...] * (1.0 - lambda_init)).astype(o_ref.dtype)


def _attend_b(q, k, v, lam_params, subln_g, *, tq, tk, pos0, l_valid, lambda_init):
    b, t, cols = q.shape
    l = k.shape[1]
    nkc = l // tk
    hd = 2 * B_HEAD_DIM
    assert tk % tq == 0 and pos0 % tq == 0 and t % tq == 0 and l % tk == 0
    k = k.reshape(b, nkc, tk, cols)
    v = v.reshape(b, nkc, tk, cols)
    slopes = jnp.broadcast_to(jnp.asarray(_alibi_slopes(B_HEADS), F32)[:, None], (B_HEADS, LANES))
    kern = functools.partial(_attn_b_kernel, tq=tq, tk=tk, pos0=pos0, l_valid=l_valid,
                             lambda_init=lambda_init)
    return pl.pallas_call(
        kern,
        grid=(b, B_HEADS, t // tq),
        in_specs=[
            pl.BlockSpec((1, tq, hd), lambda bi, h, i: (bi, i, h)),
            pl.BlockSpec((1, nkc, tk, hd), lambda bi, h, i: (bi, 0, 0, h)),
            pl.BlockSpec((1, nkc, tk, hd), lambda bi, h, i: (bi, 0, 0, h)),
            pl.BlockSpec((4, B_HEAD_DIM), lambda bi, h, i: (0, 0)),
            pl.BlockSpec((B_HEADS, LANES), lambda bi, h, i: (0, 0)),
            pl.BlockSpec((1, hd), lambda bi, h, i: (0, 0)),
        ],
        out_specs=pl.BlockSpec((1, tq, hd), lambda bi, h, i: (bi, i, h)),
        out_shape=jax.ShapeDtypeStruct((b, t, cols), BF16),
        scratch_shapes=[
            pltpu.VMEM((2 * tq, hd), BF16),
            pltpu.VMEM((2 * tq, LANES), F32),
            pltpu.VMEM((2 * tq, 2 * hd), F32),
        ],
        compiler_params=_compiler_params(("parallel", "parallel", "arbitrary")),
        name="attn_b",
    )(q, k, v, lam_params, slopes, subln_g.reshape(1, hd))


def _layer_norm(z, g, b):
    mu = jnp.mean(z, axis=1, keepdims=True)
    zc = z - mu
    var = jnp.mean(zc * zc, axis=1, keepdims=True)
    return zc * lax.rsqrt(var + LN_EPS) * g + b


def _post_kernel(x_ref, a_ref, w_ref, g_ref, b_ref, o_ref):
    mix = jnp.dot(a_ref[...], w_ref[...], preferred_element_type=F32)
    o_ref[...] = _layer_norm(ALPHA * x_ref[...] + mix, g_ref[...], b_ref[...])


def _post_mixer(x, a, w_out, g, b, tm=512):
    m, d = x.shape
    tm = _row_tile(m, tm)
    return pl.pallas_call(
        _post_kernel,
        grid=(m // tm,),
        in_specs=[
            pl.BlockSpec((tm, d), lambda i: (i, 0)),
            pl.BlockSpec((tm, a.shape[1]), lambda i: (i, 0)),
            pl.BlockSpec(w_out.shape, lambda i: (0, 0)),
            pl.BlockSpec((1, d), lambda i: (0, 0)),
            pl.BlockSpec((1, d), lambda i: (0, 0)),
        ],
        out_specs=pl.BlockSpec((tm, d), lambda i: (i, 0)),
        out_shape=jax.ShapeDtypeStruct((m, d), F32),
        compiler_params=_compiler_params(("parallel",)),
        name="post_mixer",
    )(x, a, w_out, g.reshape(1, d), b.reshape(1, d))


def _sigmoid(x):
    return 1.0 / (1.0 + jnp.exp(-x))


def _ffn_kernel(x_ref, p_ref, w1_ref, w3_ref, w2_ref, wg_ref, wp_ref, g_ref, b_ref, o_ref, acc_ref):
    f = pl.program_id(1)
    x = x_ref[...]
    xb = x.astype(BF16)

    @pl.when(f == 0)
    def _():
        gate = _sigmoid(jnp.dot(xb, wg_ref[...], preferred_element_type=F32))
        ple = jnp.dot(p_ref[...].astype(BF16), wp_ref[...], preferred_element_type=F32)
        acc_ref[...] = ALPHA * x + gate * ple

    gt = jnp.dot(xb, w1_ref[...], preferred_element_type=F32)
    up = jnp.dot(xb, w3_ref[...], preferred_element_type=F32)
    hid = (gt * _sigmoid(gt)) * up
    acc_ref[...] += jnp.dot(hid.astype(BF16), w2_ref[...], preferred_element_type=F32)

    @pl.when(f == pl.num_programs(1) - 1)
    def _():
        o_ref[...] = _layer_norm(acc_ref[...], g_ref[...], b_ref[...])


def _ffn_tile(d_ff):
    for nf in (1, 2, 3, 4, 5, 6, 7, 8, 11, 22):
        if d_ff % nf == 0 and (d_ff // nf) % LANES == 0 and d_ff // nf <= 1536:
            return d_ff // nf
    return d_ff


def _channel_and_ple(x, p, w1, w3, w2, wg, wp, g, b, tm=512):
    m, d = x.shape
    tm = _row_tile(m, tm)
    d_ff = w1.shape[1]
    tf = _ffn_tile(d_ff)
    return pl.pallas_call(
        _ffn_kernel,
        grid=(m // tm, d_ff // tf),
        in_specs=[
            pl.BlockSpec((tm, d), lambda i, f: (i, 0)),
            pl.BlockSpec((tm, p.shape[1]), lambda i, f: (i, 0)),
            pl.BlockSpec((d, tf), lambda i, f: (0, f)),
            pl.BlockSpec((d, tf), lambda i, f: (0, f)),
            pl.BlockSpec((tf, d), lambda i, f: (f, 0)),
            pl.BlockSpec(wg.shape, lambda i, f: (0, 0)),
            pl.BlockSpec(wp.shape, lambda i, f: (0, 0)),
            pl.BlockSpec((1, d), lambda i, f: (0, 0)),
            pl.BlockSpec((1, d), lambda i, f: (0, 0)),
        ],
        out_specs=pl.BlockSpec((tm, d), lambda i, f: (i, 0)),
        out_shape=jax.ShapeDtypeStruct((m, d), F32),
        scratch_shapes=[pltpu.VMEM((tm, d), F32)],
        compiler_params=_compiler_params(("parallel", "arbitrary")),
        name="ffn_ple",
    )(x, p, w1, w3, w2, wg, wp, g.reshape(1, d), b.reshape(1, d))


def _pad_keys(x, mult):
    l = x.shape[1]
    lp = -(-l // mult) * mult
    if lp == l:
        return x
    return jnp.pad(x, ((0, 0), (0, lp - l)) + ((0, 0),) * (x.ndim - 2))


def _head_slabs(w, n_heads, dim, high_half):
    k = w.shape[0]
    w3 = w.reshape(k, n_heads, dim)
    low = jnp.pad(w3, ((0, 0), (0, 0), (0, LANES - dim)))
    high = jnp.pad(w3, ((0, 0), (0, 0), (LANES - dim, 0)))
    sel = jnp.asarray(high_half, bool)[None, :, None]
    return jnp.where(sel, high, low).reshape(k, n_heads * LANES)


def _split_a_weights(w_in):
    qc = A_HEADS * A_HEAD_DIM
    kvc = A_KV_HEADS * A_HEAD_DIM
    ic = IDX_HEADS * IDX_DIM
    w_q, w_k, w_v = w_in[:, :qc], w_in[:, qc:qc + kvc], w_in[:, qc + kvc:qc + 2 * kvc]
    o = qc + 2 * kvc
    w_qi = w_in[:, o:o + ic]
    tail = w_in[:, o + ic:o + ic + IDX_DIM + IDX_HEADS]
    tail = jnp.pad(tail, ((0, 0), (0, LANES - tail.shape[1])))
    q_high = [(h // A_GROUP) % 2 == 1 for h in range(A_HEADS)]
    pieces = [_head_slabs(w_q, A_HEADS, A_HEAD_DIM, q_high), w_k, w_v,
              _head_slabs(w_qi, IDX_HEADS, IDX_DIM, [False] * IDX_HEADS), tail]
    return [p.astype(BF16) for p in pieces]


def _project_a(x2d, w_pieces):
    plan = [[(BF16, A_HEAD_DIM ** -0.5 * LOG2E)],
            [(F32, 1.0), (BF16, 1.0)],
            [(F32, 1.0), (BF16, 1.0)],
            [(BF16, 1.0)],
            [(F32, 1.0), (BF16, 1.0)]]
    return _project(x2d, w_pieces, plan)


def _mixer_a(xp, xs, cache_k, cache_v, cache_ki, w_in):
    bp, s, d = xp.shape
    bs, t, _ = xs.shape
    past = cache_k.shape[1]
    kvc = A_KV_HEADS * A_HEAD_DIM
    w_pieces = _split_a_weights(w_in)

    q, k32, k16, v32, v16, qi, kiwi32, kiwi16 = _project_a(xp.reshape(bp * s, d), w_pieces)
    n_sel_p = min(TOPK_MAX, s // 4)
    tk_p = 512 if s % 512 == 0 else s
    att_p = _attend_a(q.reshape(bp, s, -1), qi.reshape(bp, s, -1), kiwi32.reshape(bp, s, LANES),
                      kiwi16.reshape(bp, s, LANES), k16.reshape(bp, s, kvc), v16.reshape(bp, s, kvc),
                      tk=tk_p, pos0=0, l_valid=s, n_sel=n_sel_p)
    new_p = (k32.reshape(bp, s, A_KV_HEADS, A_HEAD_DIM), v32.reshape(bp, s, A_KV_HEADS, A_HEAD_DIM),
             kiwi32.reshape(bp, s, LANES)[:, :, :IDX_DIM])

    qs, k32s, k16s, v32s, v16s, qis, kiwi32s, kiwi16s = _project_a(xs.reshape(bs * t, d), w_pieces)
    l_s = past + t
    n_sel_s = min(TOPK_MAX, l_s // 4)
    tk_s = 384
    k_all = _pad_keys(jnp.concatenate([cache_k.reshape(bs, past, kvc).astype(BF16),
                                       k16s.reshape(bs, t, kvc)], axis=1), tk_s)
    v_all = _pad_keys(jnp.concatenate([cache_v.reshape(bs, past, kvc).astype(BF16),
                                       v16s.reshape(bs, t, kvc)], axis=1), tk_s)
    ki_cache = jnp.pad(cache_ki.astype(BF16), ((0, 0), (0, 0), (0, LANES - IDX_DIM)))
    ki_all = _pad_keys(jnp.concatenate([ki_cache, kiwi16s.reshape(bs, t, LANES)], axis=1), tk_s)
    att_s = _attend_a(qs.reshape(bs, t, -1), qis.reshape(bs, t, -1), kiwi32s.reshape(bs, t, LANES),
                      ki_all, k_all, v_all, tk=tk_s, pos0=past, l_valid=l_s, n_sel=n_sel_s)
    new_s = (k32s.reshape(bs, t, A_KV_HEADS, A_HEAD_DIM), v32s.reshape(bs, t, A_KV_HEADS, A_HEAD_DIM),
             kiwi32s.reshape(bs, t, LANES)[:, :, :IDX_DIM])
    return att_p.reshape(bp * s, -1), att_s.reshape(bs * t, -1), new_p, new_s


def _project_b(x2d, w_in_bf):
    cols = B_HEADS * 2 * B_HEAD_DIM
    w_pieces = [w_in_bf[:, :cols], w_in_bf[:, cols:2 * cols], w_in_bf[:, 2 * cols:]]
    plan = [[(BF16, B_HEAD_DIM ** -0.5 * LOG2E)],
            [(F32, 1.0), (BF16, 1.0)],
            [(F32, 1.0), (BF16, 1.0)]]
    return _project(x2d, w_pieces, plan)


def _mixer_b(xp, xs, cache_k, cache_v, w_in, lam_params, subln_g, lambda_init):
    bp, s, d = xp.shape
    bs, t, _ = xs.shape
    past = cache_k.shape[1]
    cols = B_HEADS * 2 * B_HEAD_DIM
    w_in_bf = w_in.astype(BF16)

    q, k32, k16, v32, v16 = _project_b(xp.reshape(bp * s, d), w_in_bf)
    tq_p = 1024 if s % 1024 == 0 else s
    tk_p = 1024 if s % 1024 == 0 else s
    att_p = _attend_b(q.reshape(bp, s, cols), k16.reshape(bp, s, cols), v16.reshape(bp, s, cols),
                      lam_params, subln_g, tq=tq_p, tk=tk_p, pos0=0, l_valid=s, lambda_init=lambda_init)
    new_p = (k32.reshape(bp, s, B_HEADS, 2 * B_HEAD_DIM), v32.reshape(bp, s, B_HEADS, 2 * B_HEAD_DIM))

    qs, k32s, k16s, v32s, v16s = _project_b(xs.reshape(bs * t, d), w_in_bf)
    l_s = past + t
    tk_s = 384
    k_all = _pad_keys(jnp.concatenate([cache_k.reshape(bs, past, cols).astype(BF16),
                                       k16s.reshape(bs, t, cols)], axis=1), tk_s)
    v_all = _pad_keys(jnp.concatenate([cache_v.reshape(bs, past, cols).astype(BF16),
                                       v16s.reshape(bs, t, cols)], axis=1), tk_s)
    att_s = _attend_b(qs.reshape(bs, t, cols), k_all, v_all, lam_params, subln_g,
                      tq=t, tk=tk_s, pos0=past, l_valid=l_s, lambda_init=lambda_init)
    new_s = (k32s.reshape(bs, t, B_HEADS, 2 * B_HEAD_DIM), v32s.reshape(bs, t, B_HEADS, 2 * B_HEAD_DIM))
    return att_p.reshape(bp * s, cols), att_s.reshape(bs * t, cols), new_p, new_s


def kernel(x_prompt, x_sample, cache_a_k, cache_a_v, cache_a_kidx, cache_b_k, cache_b_v, p_prompt, p_sample, a_w_in, a_w_out, b_w_in, b_lambda, b_subln, b_w_out, ffn_w13, ffn_w2, ple_w_proj, ple_w_gate, ln_gain, ln_bias):
    bp, s, d = x_prompt.shape
    bs, t, _ = x_sample.shape
    depth = ffn_w13.shape[0]
    d_ff = ffn_w2.shape[1]

    yp = x_prompt.reshape(bp * s, d)
    ys = x_sample.reshape(bs * t, d)
    a_new_p, a_new_s, b_new_p, b_new_s = [], [], [], []
    for i in range(depth):
        j = i // 2
        if i % 2 == 0:
            mp, ms, new_p, new_s = _mixer_a(yp.reshape(bp, s, d), ys.reshape(bs, t, d),
                                            cache_a_k[j], cache_a_v[j], cache_a_kidx[j], a_w_in[j])
            w_out = a_w_out[j].astype(BF16)
            a_new_p.append(new_p)
            a_new_s.append(new_s)
        else:
            lambda_init = 0.8 - 0.6 * math.exp(-0.3 * i)
            mp, ms, new_p, new_s = _mixer_b(yp.reshape(bp, s, d), ys.reshape(bs, t, d),
                                            cache_b_k[j], cache_b_v[j], b_w_in[j], b_lambda[j], b_subln[j],
                                            lambda_init)
            w_out = b_w_out[j].astype(BF16)
            b_new_p.append(new_p)
            b_new_s.append(new_s)
        yp = _post_mixer(yp, mp, w_out, ln_gain[i, 0], ln_bias[i, 0])
        ys = _post_mixer(ys, ms, w_out, ln_gain[i, 0], ln_bias[i, 0])
        w13 = ffn_w13[i].astype(BF16)
        w1, w3 = w13[:, :d_ff], w13[:, d_ff:]
        w2 = ffn_w2[i].astype(BF16)
        wg = ple_w_gate[i].astype(BF16)
        wp = ple_w_proj[i].astype(BF16)
        yp = _channel_and_ple(yp, p_prompt[i].reshape(bp * s, -1), w1, w3, w2, wg, wp, ln_gain[i, 1], ln_bias[i, 1])
        ys = _channel_and_ple(ys, p_sample[i].reshape(bs * t, -1), w1, w3, w2, wg, wp, ln_gain[i, 1], ln_bias[i, 1])

    def stack(items, idx):
        return jnp.stack([it[idx] for it in items])

    return (yp.reshape(bp, s, d), ys.reshape(bs, t, d),
            stack(a_new_p, 0), stack(a_new_p, 1), stack(a_new_p, 2),
            stack(b_new_p, 0), stack(b_new_p, 1),
            stack(a_new_s, 0), stack(a_new_s, 1), stack(a_new_s, 2),
            stack(b_new_s, 0), stack(b_new_s, 1))
```

```python
import functools
import math

import jax
import jax.numpy as jnp
import ml_dtypes
import numpy as np
from jax import lax
from jax.experimental import pallas as pl
from jax.experimental.pallas import tpu as pltpu

F32 = jnp.float32
BF16 = jnp.bfloat16
I32 = jnp.int32

CHUNK = 64
CHUNK_SHIFT = 6
A_HEADS = 16
A_HEAD_DIM = 64
A_KV_HEADS = 4
A_GROUP = A_HEADS // A_KV_HEADS
IDX_HEADS = 8
IDX_DIM = 64
TOPK_MAX = 256
IDX_SCALE = IDX_DIM ** -0.5 * IDX_HEADS ** -0.5
B_HEADS = 8
B_HEAD_DIM = 64
DEPTH = 2
ALPHA = (2 * DEPTH) ** 0.25
LN_EPS = 1e-5

LANES = 128
HALF = LANES // 2
VMEM_LIMIT_BYTES = 56 * 1024 * 1024

NEG_FILL = -1e30
M_INIT = -5e29
INT_MIN = -(2 ** 31)
KEY_NEG_INF = int(np.int32(np.uint32(0xFF800000) ^ np.uint32(0x7FFFFFFF)))
POS_BIG = 2 ** 30

B_ROW_GROUP = 256
COUNT_ROWS = 64
BITS_PER_CHECK = 4
LOG2E = float(np.log2(np.e))
NT_DIMS = (((1,), (1,)), ((), ()))


def _alibi_slopes(n_heads):
    return [float(2.0 ** (-8.0 * (h + 1) / n_heads)) for h in range(n_heads)]


def _compiler_params(semantics):
    return pltpu.CompilerParams(dimension_semantics=semantics,
                                vmem_limit_bytes=VMEM_LIMIT_BYTES)


def _row_tile(m, want):
    return want if m % want == 0 else m


def _block_geometry(pos0, i, tq, tk, l_valid):
    q0 = pos0 + i * tq
    qpos = q0 + lax.broadcasted_iota(I32, (tq, 1), 0)
    limit = jnp.minimum((lax.shift_right_logical(qpos, CHUNK_SHIFT) + 1) * CHUNK, l_valid)
    lim_max = jnp.minimum((lax.shift_right_logical(q0 + tq - 1, CHUNK_SHIFT) + 1) * CHUNK, l_valid)
    return qpos, limit, (lim_max + tk - 1) // tk


def _proj_kernel(x_ref, *refs, out_plan):
    n_w = len(out_plan)
    w_refs, o_refs = refs[:n_w], refs[n_w:]
    x = x_ref[...].astype(BF16)
    oi = 0
    for w_ref, outs in zip(w_refs, out_plan):
        y = jnp.dot(x, w_ref[...], preferred_element_type=F32)
        for dtype, scale in outs:
            o_refs[oi][...] = (y if scale == 1.0 else y * scale).astype(dtype)
            oi += 1


def _project(x, weights, out_plan, tm=512):
    m, k = x.shape
    tm = _row_tile(m, tm)
    in_specs = [pl.BlockSpec((tm, k), lambda i: (i, 0))]
    out_shape, out_specs = [], []
    for w, outs in zip(weights, out_plan):
        n = w.shape[1]
        in_specs.append(pl.BlockSpec((k, n), lambda i: (0, 0)))
        for dtype, _ in outs:
            out_shape.append(jax.ShapeDtypeStruct((m, n), dtype))
            out_specs.append(pl.BlockSpec((tm, n), lambda i: (i, 0)))
    return pl.pallas_call(
        functools.partial(_proj_kernel, out_plan=out_plan),
        grid=(m // tm,),
        in_specs=in_specs,
        out_specs=out_specs,
        out_shape=out_shape,
        compiler_params=_compiler_params(("parallel",)),
        name="proj",
    )(x, *weights)


def _row_fold(x, rows):
    part = x[0:rows]
    for s in range(1, x.shape[0] // rows):
        part = part + x[s * rows:(s + 1) * rows]
    return part


def _split_bf16(x):
    hi = float(np.float32(x).astype(ml_dtypes.bfloat16))
    return hi, x - hi


def _attn_a_kernel(q_ref, qi_ref, wq_ref, kiwi_ref, k_ref, v_ref, o_ref,
                   key_ref, pf_ref, qs_ref, qis_ref, m_ref, acc_ref,
                   *, tq, tk, pos0, l_valid, t_valid, n_sel, pos_bits):
    i = pl.program_id(1)
    nslab = tk // LANES
    qpos, limit, nk = _block_geometry(pos0, i, tq, tk, l_valid)
    lane = lax.broadcasted_iota(I32, (1, LANES), 1)
    qpos_l = pos0 + i * tq + lax.broadcasted_iota(I32, (1, tq), 1)
    limit_l = jnp.minimum((lax.shift_right_logical(qpos_l, CHUNK_SHIFT) + 1) * CHUNK, l_valid)
    row_ok = lax.broadcasted_iota(I32, (1, tq), 1) < t_valid
    slopes2 = [s * LOG2E for s in _alibi_slopes(A_HEADS)]

    eye = jnp.where(lax.broadcasted_iota(I32, (tq, LANES), 0) == lax.broadcasted_iota(I32, (tq, LANES), 1),
                    1.0, 0.0).astype(BF16)
    lane_q = lax.broadcasted_iota(I32, (tq, LANES), 1)
    for c in range(A_KV_HEADS):
        fb = (1 - c % 2) * HALF
        for g in range(A_GROUP):
            h = c * A_GROUP + g
            s_hi, s_lo = _split_bf16(slopes2[h])
            coef = jnp.where(lane_q == fb, 32.0 * s_hi,
                             jnp.where(lane_q == fb + 1, s_hi,
                                       jnp.where(lane_q == fb + 2, 32.0 * s_lo, s_lo))).astype(BF16)
            is_coef = (lane_q >= fb) & (lane_q < fb + 4)
            rows = slice(g * tq, (g + 1) * tq)
            qs_ref[c, rows, 0:LANES] = jnp.where(is_coef, coef, q_ref[0, :, h * LANES:(h + 1) * LANES])
            qs_ref[c, rows, LANES:2 * LANES] = eye
    for h in range(IDX_HEADS):
        qis_ref[h * tq:(h + 1) * tq, :] = qi_ref[0, :, h * LANES:(h + 1) * LANES]
    krow = lax.broadcasted_iota(I32, (2 * tk, LANES), 0)
    r_hi = lax.shift_right_logical(krow, 5).astype(F32)
    r_lo = (krow & 31).astype(F32)
    lane_2k = lax.broadcasted_iota(I32, (2 * tk, LANES), 1)
    lane_k = lax.broadcasted_iota(I32, (tk, LANES), 1)
    for par in range(2):
        fb = (1 - par) * HALF
        pf_ref[par] = jnp.where((lane_2k == fb) | (lane_2k == fb + 2), r_hi,
                                jnp.where((lane_2k == fb + 1) | (lane_2k == fb + 3), r_lo, 0.0)).astype(BF16)

    w_t = jnp.transpose(wq_ref[0])[IDX_DIM:IDX_DIM + IDX_HEADS] * IDX_SCALE

    half_rows = IDX_HEADS // 2 * tq

    def score_body(j, carry):
        kic = kiwi_ref[0, j]
        ds = [lax.dot_general(kic, qis_ref[r * half_rows:(r + 1) * half_rows, :], NT_DIMS,
                              preferred_element_type=F32) for r in range(2)]
        sc = None
        for h in range(IDX_HEADS):
            d = ds[h // 4][:, (h % 4) * tq:(h % 4 + 1) * tq]
            term = jnp.maximum(d, 0.0) * w_t[h:h + 1]
            sc = term if sc is None else sc + term
        kpos = j * tk + lax.broadcasted_iota(I32, (tk, tq), 0)
        sc = jnp.where(kpos < limit_l, sc, -jnp.inf)
        bits = lax.bitcast_convert_type(sc, I32)
        key = bits ^ (lax.shift_right_arithmetic(bits, 31) & 0x7FFFFFFF)
        key_ref[j] = jnp.where(bits == INT_MIN, 0, key)
        return carry

    lax.fori_loop(0, nk, score_body, 0)

    def count(pred):
        def body(j, acc):
            kpos = j * tk + lax.broadcasted_iota(I32, (tk, tq), 0)
            return acc + _row_fold(jnp.where(pred(key_ref[j], kpos), 1.0, 0.0), COUNT_ROWS)
        acc = lax.fori_loop(0, nk, body, jnp.zeros((COUNT_ROWS, tq), F32))
        return jnp.sum(acc, axis=0, keepdims=True)

    k_f = float(n_sel)
    n_all = (jnp.zeros((1, tq), I32) + nk * tk).astype(F32)
    c_nonneg = count(lambda u, kp: u >= 0)
    take = c_nonneg >= k_f
    thr0 = jnp.where(take, 0, INT_MIN).astype(I32)
    c_ge0 = jnp.where(take, c_nonneg, n_all)

    def unsettled(c_ge):
        return jnp.sum(jnp.where((c_ge != k_f) & row_ok, 1.0, 0.0)) > 0.0

    def bits_cond(carry):
        grp, _, c_ge = carry
        return (grp * BITS_PER_CHECK < 31) & unsettled(c_ge)

    def bits_body(carry):
        grp, thr, c_ge = carry
        for bb in range(BITS_PER_CHECK):
            b = grp * BITS_PER_CHECK + bb
            bit = jnp.where(b <= 30, lax.shift_left(jnp.int32(1), 30 - jnp.minimum(b, 30)), 0)
            cand = thr | bit
            c = count(lambda u, kp: u >= cand)
            take = c >= k_f
            thr, c_ge = jnp.where(take, cand, thr), jnp.where(take, c, c_ge)
        return grp + 1, thr, c_ge

    _, thr, c_ge = lax.while_loop(bits_cond, bits_body, (jnp.int32(0), thr0, c_ge0))

    surplus = jnp.where((c_ge > k_f) & (thr != KEY_NEG_INF) & row_ok, 1.0, 0.0)
    any_surplus = jnp.sum(surplus) > 0.0

    def tie_cut():
        need = k_f - count(lambda u, kp: u > thr)

        def body(b, x):
            cand = x | lax.shift_left(jnp.int32(1), pos_bits - 1 - b)
            c = count(lambda u, kp: (u == thr) & (kp < cand))
            return jnp.where(c < need, cand, x)
        return lax.fori_loop(0, pos_bits, body, jnp.zeros((1, tq), I32))

    cut = lax.cond(any_surplus, tie_cut, lambda: jnp.full((1, tq), POS_BIG, I32))

    m_ref[...] = jnp.full(m_ref.shape, M_INIT, F32)
    acc_ref[...] = jnp.zeros(acc_ref.shape, F32)

    def process_chunks(j0, n, diag):
        base = jnp.full((1, LANES), j0 * tk, I32).astype(F32)
        masks, k_alls, v_alls = [], [], []
        for d in range(n):
            u = key_ref[j0 + d]
            kpos = (j0 + d) * tk + lax.broadcasted_iota(I32, (tk, tq), 0)
            sel = (u > thr) | ((u == thr) & (kpos <= cut))
            if diag and d == n - 1:
                sel = sel & (kpos < limit_l)
            masks.append(jnp.where(sel, 0.0, NEG_FILL).astype(BF16))
            k_alls.append(k_ref[0, j0 + d])
            v_alls.append(v_ref[0, j0 + d])
        rhs, vvs = [], []
        for c in range(A_KV_HEADS):
            sl = slice((c // 2) * LANES, (c // 2 + 1) * LANES)
            keep = (lane_k < HALF) if c % 2 == 0 else (lane_k >= HALF)
            rhs.append(jnp.concatenate(
                [jnp.concatenate([jnp.where(keep, k_alls[d][:, sl], pf_ref[c % 2, d * tk:(d + 1) * tk]),
                                  masks[d]], axis=1) for d in range(n)], axis=0))
            vvs.append(jnp.concatenate(
                [jnp.where(keep, v_alls[d][:, sl], jnp.ones_like(v_alls[d][:, sl])) for d in range(n)],
                axis=0))
        ss = [lax.dot_general(qs_ref[c], rhs[c], NT_DIMS, preferred_element_type=F32)
              for c in range(A_KV_HEADS)]

        slabs = range(n * nslab)
        if diag:
            qrel = qpos - j0 * tk
            future = {sb: 2.0 * jnp.maximum((lane + sb * LANES) - qrel, 0).astype(F32)
                      for sb in slabs if sb >= (n - 1) * nslab}

        for c in range(A_KV_HEADS):
            ps, alphas = [], []
            for g in range(A_GROUP):
                slope = slopes2[c * A_GROUP + g]
                rows = slice(g * tq, (g + 1) * tq)
                ts = [ss[c][rows, sb * LANES:(sb + 1) * LANES] for sb in slabs]
                if diag:
                    ts = [t - slope * future[sb] if sb in future else t for sb, t in zip(slabs, ts)]
                c_j = slope * base
                mx = ts[0]
                for t in ts[1:]:
                    mx = jnp.maximum(mx, t)
                m_old = m_ref[c, rows, :]
                m_new = jnp.maximum(m_old, jnp.max(mx, axis=1, keepdims=True) + c_j)
                ref = m_new - c_j
                alphas.append(jnp.exp2(m_old - m_new))
                ps.append(jnp.concatenate([jnp.exp2(t - ref).astype(BF16) for t in ts], axis=1))
                m_ref[c, rows, :] = m_new
            pv = jnp.dot(jnp.concatenate(ps, axis=0), vvs[c], preferred_element_type=F32)
            acc_ref[c] = acc_ref[c] * jnp.concatenate(alphas, axis=0) + pv

    n_past = nk - 1

    def past_pair(jj, carry):
        process_chunks(2 * jj, 2, diag=False)
        return carry

    lax.fori_loop(0, n_past // 2, past_pair, 0)

    @pl.when(n_past % 2 == 1)
    def _():
        process_chunks(nk - 2, 2, diag=True)

    @pl.when(n_past % 2 == 0)
    def _():
        process_chunks(nk - 1, 1, diag=True)

    for c in range(A_KV_HEADS):
        for g in range(A_GROUP):
            h = c * A_GROUP + g
            a = acc_ref[c, g * tq:(g + 1) * tq, :]
            val = a / pltpu.roll(a, HALF, axis=1)
            off = (c % 2) * HALF
            o_ref[0, :, h * A_HEAD_DIM:(h + 1) * A_HEAD_DIM] = val[:, off:off + HALF].astype(o_ref.dtype)


def _attend_a(q, qi, kiwi_q, kiwi_k, k, v, *, tk, pos0, l_valid, n_sel):
    b, t, _ = q.shape
    tq = LANES
    t_pad = -(-t // tq) * tq
    if t_pad != t:
        q, qi, kiwi_q = (jnp.pad(a, ((0, 0), (0, t_pad - t), (0, 0))) for a in (q, qi, kiwi_q))
    l = k.shape[1]
    nkc = l // tk
    assert tk % tq == 0 and pos0 % tq == 0 and l % tk == 0 and 2 * tk <= 32 * 256 and pos0 + t_pad <= l
    pos_bits = max(1, int(math.ceil(math.log2(l + 1))))
    kvc = A_KV_HEADS * A_HEAD_DIM
    kiwi_k = kiwi_k.reshape(b, nkc, tk, LANES)
    k = k.reshape(b, nkc, tk, kvc)
    v = v.reshape(b, nkc, tk, kvc)
    kern = functools.partial(_attn_a_kernel, tq=tq, tk=tk, pos0=pos0, l_valid=l_valid, t_valid=t,
                             n_sel=n_sel, pos_bits=pos_bits)
    ocols = A_HEADS * A_HEAD_DIM
    out = pl.pallas_call(
        kern,
        grid=(b, t_pad // tq),
        in_specs=[
            pl.BlockSpec((1, tq, A_HEADS * LANES), lambda bi, i: (bi, i, 0)),
            pl.BlockSpec((1, tq, IDX_HEADS * LANES), lambda bi, i: (bi, i, 0)),
            pl.BlockSpec((1, tq, LANES), lambda bi, i: (bi, i, 0)),
            pl.BlockSpec((1, nkc, tk, LANES), lambda bi, i: (bi, 0, 0, 0)),
            pl.BlockSpec((1, nkc, tk, kvc), lambda bi, i: (bi, 0, 0, 0)),
            pl.BlockSpec((1, nkc, tk, kvc), lambda bi, i: (bi, 0, 0, 0)),
        ],
        out_specs=pl.BlockSpec((1, tq, ocols), lambda bi, i: (bi, i, 0)),
        out_shape=jax.ShapeDtypeStruct((b, t_pad, ocols), BF16),
        scratch_shapes=[
            pltpu.VMEM((nkc, tk, tq), I32),
            pltpu.VMEM((2, 2 * tk, LANES), BF16),
            pltpu.VMEM((A_KV_HEADS, A_GROUP * tq, 2 * LANES), BF16),
            pltpu.VMEM((IDX_HEADS * tq, LANES), BF16),
            pltpu.VMEM((A_KV_HEADS, A_GROUP * tq, LANES), F32),
            pltpu.VMEM((A_KV_HEADS, A_GROUP * tq, LANES), F32),
        ],
        compiler_params=_compiler_params(("parallel", "arbitrary")),
        name="attn_a",
    )(q, qi, kiwi_q, kiwi_k, k, v)
    return out[:, :t]


def _attn_b_kernel(q_ref, k_ref, v_ref, lam_ref, slope_ref, g_ref, o_ref,
                   qq_ref, m_ref, acc_ref, *, tq, tk, pos0, l_valid, lambda_init):
    h = pl.program_id(1)
    i = pl.program_id(2)
    nslab = tk // LANES
    qpos, limit, nk = _block_geometry(pos0, i, tq, tk, l_valid)
    slope2 = slope_ref[pl.ds(h, 1), :] * LOG2E
    lane = lax.broadcasted_iota(I32, (1, LANES), 1)

    q = q_ref[0]
    lane_q = lax.broadcasted_iota(I32, (tq, LANES), 1)
    zero = jnp.zeros_like(q)
    qq_ref[0:tq, :] = jnp.where(lane_q < B_HEAD_DIM, q, zero)
    qq_ref[tq:2 * tq, :] = jnp.where(lane_q >= B_HEAD_DIM, q, zero)

    m_ref[...] = jnp.full(m_ref.shape, M_INIT, F32)
    acc_ref[...] = jnp.zeros(acc_ref.shape, F32)

    def finish_rows(j, rs, ts, vv):
        c_j = slope2 * (j * tk).astype(F32)
        mx = ts[0]
        for t in ts[1:]:
            mx = jnp.maximum(mx, t)
        m_old = m_ref[rs, :]
        m_new = jnp.maximum(m_old, jnp.max(mx, axis=1, keepdims=True) + c_j)
        ref = m_new - c_j
        alpha = jnp.exp2(m_old - m_new)
        p = jnp.concatenate([jnp.exp2(t - ref).astype(BF16) for t in ts], axis=1)
        pv = jnp.dot(p, vv, preferred_element_type=F32)
        acc_ref[rs, :] = acc_ref[rs, :] * jnp.concatenate([alpha, alpha], axis=1) + pv
        m_ref[rs, :] = m_new

    def chunk_operands(j):
        vc = v_ref[0, j]
        return k_ref[0, j], jnp.concatenate([vc, jnp.ones_like(vc)], axis=1)

    rg = min(tq, B_ROW_GROUP)
    groups = [slice(r, r + rg) for r in range(0, 2 * tq, rg)]

    def visible_chunk(j, carry):
        kc, vv = chunk_operands(j)
        ss = [lax.dot_general(qq_ref[rs, :], kc, NT_DIMS, preferred_element_type=F32) for rs in groups]
        for rs, s in zip(groups, ss):
            ts = []
            for g in range(nslab):
                kb = slope2 * (lane + g * LANES).astype(F32)
                ts.append(s[:, g * LANES:(g + 1) * LANES] + kb)
            finish_rows(j, rs, ts, vv)
        return carry

    lax.fori_loop(0, nk - 1, visible_chunk, 0)

    j = nk - 1
    kc, vv = chunk_operands(j)
    qrel = qpos - j * tk
    qrel_f = qrel.astype(F32)
    lim_rel = limit - j * tk
    block_starts_chunk = tq == tk and pos0 % tk == 0
    plans = []
    for rs in groups:
        q_lo = rs.start % tq
        n_vis = (q_lo + rg) // LANES if block_starts_chunk else nslab
        n_past = q_lo // LANES if block_starts_chunk else 0
        plans.append((rs, slice(q_lo, q_lo + rg), n_vis, n_past))
    ss = [lax.dot_general(qq_ref[rs, :], kc[0:n_vis * LANES], NT_DIMS, preferred_element_type=F32)
          for rs, _, n_vis, _ in plans]
    for (rs, qr, n_vis, n_past), s in zip(plans, ss):
        ts = []
        for g in range(n_vis):
            krel = lane + g * LANES
            sg = s[:, g * LANES:(g + 1) * LANES]
            if g < n_past:
                ts.append(sg + slope2 * krel.astype(F32))
            else:
                bias = slope2 * (jnp.abs((qrel[qr] - krel).astype(F32)) - qrel_f[qr])
                ts.append(jnp.where(krel < lim_rel[qr], sg - bias, NEG_FILL))
        finish_rows(j, rs, ts, vv[0:n_vis * LANES])

    hd = 2 * B_HEAD_DIM
    lp = lam_ref[...]
    lam = (jnp.exp(jnp.sum(lp[0:1] * lp[1:2], axis=1, keepdims=True))
           - jnp.exp(jnp.sum(lp[2:3] * lp[3:4], axis=1, keepdims=True)) + lambda_init)
    out = (acc_ref[0:tq, 0:hd] / acc_ref[0:tq, hd:2 * hd]
           - lam * (acc_ref[tq:2 * tq, 0:hd] / acc_ref[tq:2 * tq, hd:2 * hd]))
    rms = lax.rsqrt(jnp.mean(out * out, axis=1, keepdims=True) + LN_EPS)
    o_ref[0] = (out * rms * g_ref[...] * (1.0 - lambda_init)).astype(o_ref.dtype)


def _attend_b(q, k, v, lam_params, subln_g, *, tq, tk, pos0, l_valid, lambda_init):
    b, t, cols = q.shape
    l = k.shape[1]
    nkc = l // tk
    hd = 2 * B_HEAD_DIM
    assert tk % tq == 0 and pos0 % tq == 0 and t % tq == 0 and l % tk == 0
    k = k.reshape(b, nkc, tk, cols)
    v = v.reshape(b, nkc, tk, cols)
    slopes = jnp.broadcast_to(jnp.asarray(_alibi_slopes(B_HEADS), F32)[:, None], (B_HEADS, LANES))
    kern = functools.partial(_attn_b_kernel, tq=tq, tk=tk, pos0=pos0, l_valid=l_valid,
                             lambda_init=lambda_init)
    return pl.pallas_call(
        kern,
        grid=(b, B_HEADS, t // tq),
        in_specs=[
            pl.BlockSpec((1, tq, hd), lambda bi, h, i: (bi, i, h)),
            pl.BlockSpec((1, nkc, tk, hd), lambda bi, h, i: (bi, 0, 0, h)),
            pl.BlockSpec((1, nkc, tk, hd), lambda bi, h, i: (bi, 0, 0, h)),
            pl.BlockSpec((4, B_HEAD_DIM), lambda bi, h, i: (0, 0)),
            pl.BlockSpec((B_HEADS, LANES), lambda bi, h, i: (0, 0)),
            pl.BlockSpec((1, hd), lambda bi, h, i: (0, 0)),
        ],
        out_specs=pl.BlockSpec((1, tq, hd), lambda bi, h, i: (bi, i, h)),
        out_shape=jax.ShapeDtypeStruct((b, t, cols), BF16),
        scratch_shapes=[
            pltpu.VMEM((2 * tq, hd), BF16),
            pltpu.VMEM((2 * tq, LANES), F32),
            pltpu.VMEM((2 * tq, 2 * hd), F32),
        ],
        compiler_params=_compiler_params(("parallel", "parallel", "arbitrary")),
        name="attn_b",
    )(q, k, v, lam_params, slopes, subln_g.reshape(1, hd))


def _layer_norm(z, g, b):
    mu = jnp.mean(z, axis=1, keepdims=True)
    zc = z - mu
    var = jnp.mean(zc * zc, axis=1, keepdims=True)
    return zc * lax.rsqrt(var + LN_EPS) * g + b


def _post_kernel(x_ref, a_ref, w_ref, g_ref, b_ref, o_ref):
    mix = jnp.dot(a_ref[...], w_ref[...], preferred_element_type=F32)
    o_ref[...] = _layer_norm(ALPHA * x_ref[...] + mix, g_ref[...], b_ref[...])


def _post_mixer(x, a, w_out, g, b, tm=512):
    m, d = x.shape
    tm = _row_tile(m, tm)
    return pl.pallas_call(
        _post_kernel,
        grid=(m // tm,),
        in_specs=[
            pl.BlockSpec((tm, d), lambda i: (i, 0)),
            pl.BlockSpec((tm, a.shape[1]), lambda i: (i, 0)),
            pl.BlockSpec(w_out.shape, lambda i: (0, 0)),
            pl.BlockSpec((1, d), lambda i: (0, 0)),
            pl.BlockSpec((1, d), lambda i: (0, 0)),
        ],
        out_specs=pl.BlockSpec((tm, d), lambda i: (i, 0)),
        out_shape=jax.ShapeDtypeStruct((m, d), F32),
        compiler_params=_compiler_params(("parallel",)),
        name="post_mixer",
    )(x, a, w_out, g.reshape(1, d), b.reshape(1, d))


def _sigmoid(x):
    return 1.0 / (1.0 + jnp.exp(-x))


def _ffn_kernel(x_ref, p_ref, w1_ref, w3_ref, w2_ref, wg_ref, wp_ref, g_ref, b_ref, o_ref, acc_ref):
    f = pl.program_id(1)
    x = x_ref[...]
    xb = x.astype(BF16)

    @pl.when(f == 0)
    def _():
        gate = _sigmoid(jnp.dot(xb, wg_ref[...], preferred_element_type=F32))
        ple = jnp.dot(p_ref[...].astype(BF16), wp_ref[...], preferred_element_type=F32)
        acc_ref[...] = ALPHA * x + gate * ple

    gt = jnp.dot(xb, w1_ref[...], preferred_element_type=F32)
    up = jnp.dot(xb, w3_ref[...], preferred_element_type=F32)
    hid = (gt * _sigmoid(gt)) * up
    acc_ref[...] += jnp.dot(hid.astype(BF16), w2_ref[...], preferred_element_type=F32)

    @pl.when(f == pl.num_programs(1) - 1)
    def _():
        o_ref[...] = _layer_norm(acc_ref[...], g_ref[...], b_ref[...])


def _ffn_tile(d_ff):
    for nf in (1, 2, 3, 4, 5, 6, 7, 8, 11, 22):
        if d_ff % nf == 0 and (d_ff // nf) % LANES == 0 and d_ff // nf <= 1536:
            return d_ff // nf
    return d_ff


def _channel_and_ple(x, p, w1, w3, w2, wg, wp, g, b, tm=512):
    m, d = x.shape
    tm = _row_tile(m, tm)
    d_ff = w1.shape[1]
    tf = _ffn_tile(d_ff)
    return pl.pallas_call(
        _ffn_kernel,
        grid=(m // tm, d_ff // tf),
        in_specs=[
            pl.BlockSpec((tm, d), lambda i, f: (i, 0)),
            pl.BlockSpec((tm, p.shape[1]), lambda i, f: (i, 0)),
            pl.BlockSpec((d, tf), lambda i, f: (0, f)),
            pl.BlockSpec((d, tf), lambda i, f: (0, f)),
            pl.BlockSpec((tf, d), lambda i, f: (f, 0)),
            pl.BlockSpec(wg.shape, lambda i, f: (0, 0)),
            pl.BlockSpec(wp.shape, lambda i, f: (0, 0)),
            pl.BlockSpec((1, d), lambda i, f: (0, 0)),
            pl.BlockSpec((1, d), lambda i, f: (0, 0)),
        ],
        out_specs=pl.BlockSpec((tm, d), lambda i, f: (i, 0)),
        out_shape=jax.ShapeDtypeStruct((m, d), F32),
        scratch_shapes=[pltpu.VMEM((tm, d), F32)],
        compiler_params=_compiler_params(("parallel", "arbitrary")),
        name="ffn_ple",
    )(x, p, w1, w3, w2, wg, wp, g.reshape(1, d), b.reshape(1, d))


def _pad_keys(x, mult):
    l = x.shape[1]
    lp = -(-l // mult) * mult
    if lp == l:
        return x
    return jnp.pad(x, ((0, 0), (0, lp - l)) + ((0, 0),) * (x.ndim - 2))


def _head_slabs(w, n_heads, dim, high_half):
    k = w.shape[0]
    w3 = w.reshape(k, n_heads, dim)
    low = jnp.pad(w3, ((0, 0), (0, 0), (0, LANES - dim)))
    high = jnp.pad(w3, ((0, 0), (0, 0), (LANES - dim, 0)))
    sel = jnp.asarray(high_half, bool)[None, :, None]
    return jnp.where(sel, high, low).reshape(k, n_heads * LANES)


def _split_a_weights(w_in):
    qc = A_HEADS * A_HEAD_DIM
    kvc = A_KV_HEADS * A_HEAD_DIM
    ic = IDX_HEADS * IDX_DIM
    w_q, w_k, w_v = w_in[:, :qc], w_in[:, qc:qc + kvc], w_in[:, qc + kvc:qc + 2 * kvc]
    o = qc + 2 * kvc
    w_qi = w_in[:, o:o + ic]
    tail = w_in[:, o + ic:o + ic + IDX_DIM + IDX_HEADS]
    tail = jnp.pad(tail, ((0, 0), (0, LANES - tail.shape[1])))
    q_high = [(h // A_GROUP) % 2 == 1 for h in range(A_HEADS)]
    pieces = [_head_slabs(w_q, A_HEADS, A_HEAD_DIM, q_high), w_k, w_v,
              _head_slabs(w_qi, IDX_HEADS, IDX_DIM, [False] * IDX_HEADS), tail]
    return [p.astype(BF16) for p in pieces]


def _project_a(x2d, w_pieces):
    plan = [[(BF16, A_HEAD_DIM ** -0.5 * LOG2E)],
            [(F32, 1.0), (BF16, 1.0)],
            [(F32, 1.0), (BF16, 1.0)],
            [(BF16, 1.0)],
            [(F32, 1.0), (BF16, 1.0)]]
    return _project(x2d, w_pieces, plan)


def _mixer_a(xp, xs, cache_k, cache_v, cache_ki, w_in):
    bp, s, d = xp.shape
    bs, t, _ = xs.shape
    past = cache_k.shape[1]
    kvc = A_KV_HEADS * A_HEAD_DIM
    w_pieces = _split_a_weights(w_in)

    q, k32, k16, v32, v16, qi, kiwi32, kiwi16 = _project_a(xp.reshape(bp * s, d), w_pieces)
    n_sel_p = min(TOPK_MAX, s // 4)
    tk_p = 512 if s % 512 == 0 else s
    att_p = _attend_a(q.reshape(bp, s, -1), qi.reshape(bp, s, -1), kiwi32.reshape(bp, s, LANES),
                      kiwi16.reshape(bp, s, LANES), k16.reshape(bp, s, kvc), v16.reshape(bp, s, kvc),
                      tk=tk_p, pos0=0, l_valid=s, n_sel=n_sel_p)
    new_p = (k32.reshape(bp, s, A_KV_HEADS, A_HEAD_DIM), v32.reshape(bp, s, A_KV_HEADS, A_HEAD_DIM),
             kiwi32.reshape(bp, s, LANES)[:, :, :IDX_DIM])

    qs, k32s, k16s, v32s, v16s, qis, kiwi32s, kiwi16s = _project_a(xs.reshape(bs * t, d), w_pieces)
    l_s = past + t
    n_sel_s = min(TOPK_MAX, l_s // 4)
    tk_s = 384
    k_all = _pad_keys(jnp.concatenate([cache_k.reshape(bs, past, kvc).astype(BF16),
                                       k16s.reshape(bs, t, kvc)], axis=1), tk_s)
    v_all = _pad_keys(jnp.concatenate([cache_v.reshape(bs, past, kvc).astype(BF16),
                                       v16s.reshape(bs, t, kvc)], axis=1), tk_s)
    ki_cache = jnp.pad(cache_ki.astype(BF16), ((0, 0), (0, 0), (0, LANES - IDX_DIM)))
    ki_all = _pad_keys(jnp.concatenate([ki_cache, kiwi16s.reshape(bs, t, LANES)], axis=1), tk_s)
    att_s = _attend_a(qs.reshape(bs, t, -1), qis.reshape(bs, t, -1), kiwi32s.reshape(bs, t, LANES),
                      ki_all, k_all, v_all, tk=tk_s, pos0=past, l_valid=l_s, n_sel=n_sel_s)
    new_s = (k32s.reshape(bs, t, A_KV_HEADS, A_HEAD_DIM), v32s.reshape(bs, t, A_KV_HEADS, A_HEAD_DIM),
             kiwi32s.reshape(bs, t, LANES)[:, :, :IDX_DIM])
    return att_p.reshape(bp * s, -1), att_s.reshape(bs * t, -1), new_p, new_s


def _project_b(x2d, w_in_bf):
    cols = B_HEADS * 2 * B_HEAD_DIM
    w_pieces = [w_in_bf[:, :cols], w_in_bf[:, cols:2 * cols], w_in_bf[:, 2 * cols:]]
    plan = [[(BF16, B_HEAD_DIM ** -0.5 * LOG2E)],
            [(F32, 1.0), (BF16, 1.0)],
            [(F32, 1.0), (BF16, 1.0)]]
    return _project(x2d, w_pieces, plan)


def _mixer_b(xp, xs, cache_k, cache_v, w_in, lam_params, subln_g, lambda_init):
    bp, s, d = xp.shape
    bs, t, _ = xs.shape
    past = cache_k.shape[1]
    cols = B_HEADS * 2 * B_HEAD_DIM
    w_in_bf = w_in.astype(BF16)

    q, k32, k16, v32, v16 = _project_b(xp.reshape(bp * s, d), w_in_bf)
    tq_p = 1024 if s % 1024 == 0 else s
    tk_p = 1024 if s % 1024 == 0 else s
    att_p = _attend_b(q.reshape(bp, s, cols), k16.reshape(bp, s, cols), v16.reshape(bp, s, cols),
                      lam_params, subln_g, tq=tq_p, tk=tk_p, pos0=0, l_valid=s, lambda_init=lambda_init)
    new_p = (k32.reshape(bp, s, B_HEADS, 2 * B_HEAD_DIM), v32.reshape(bp, s, B_HEADS, 2 * B_HEAD_DIM))

    qs, k32s, k16s, v32s, v16s = _project_b(xs.reshape(bs * t, d), w_in_bf)
    l_s = past + t
    tk_s = 384
    k_all = _pad_keys(jnp.concatenate([cache_k.reshape(bs, past, cols).astype(BF16),
                                       k16s.reshape(bs, t, cols)], axis=1), tk_s)
    v_all = _pad_keys(jnp.concatenate([cache_v.reshape(bs, past, cols).astype(BF16),
                                       v16s.reshape(bs, t, cols)], axis=1), tk_s)
    att_s = _attend_b(qs.reshape(bs, t, cols), k_all, v_all, lam_params, subln_g,
                      tq=t, tk=tk_s, pos0=past, l_valid=l_s, lambda_init=lambda_init)
    new_s = (k32s.reshape(bs, t, B_HEADS, 2 * B_HEAD_DIM), v32s.reshape(bs, t, B_HEADS, 2 * B_HEAD_DIM))
    return att_p.reshape(bp * s, cols), att_s.reshape(bs * t, cols), new_p, new_s


def kernel(x_prompt, x_sample, cache_a_k, cache_a_v, cache_a_kidx, cache_b_k, cache_b_v, p_prompt, p_sample, a_w_in, a_w_out, b_w_in, b_lambda, b_subln, b_w_out, ffn_w13, ffn_w2, ple_w_proj, ple_w_gate, ln_gain, ln_bias):
    bp, s, d = x_prompt.shape
    bs, t, _ = x_sample.shape
    depth = ffn_w13.shape[0]
    d_ff = ffn_w2.shape[1]

    yp = x_prompt.reshape(bp * s, d)
    ys = x_sample.reshape(bs * t, d)
    a_new_p, a_new_s, b_new_p, b_new_s = [], [], [], []
    for i in range(depth):
        j = i // 2
        if i % 2 == 0:
            mp, ms, new_p, new_s = _mixer_a(yp.reshape(bp, s, d), ys.reshape(bs, t, d),
                                            cache_a_k[j], cache_a_v[j], cache_a_kidx[j], a_w_in[j])
            w_out = a_w_out[j].astype(BF16)
            a_new_p.append(new_p)
            a_new_s.append(new_s)
        else:
            lambda_init = 0.8 - 0.6 * math.exp(-0.3 * i)
            mp, ms, new_p, new_s = _mixer_b(yp.reshape(bp, s, d), ys.reshape(bs, t, d),
                                            cache_b_k[j], cache_b_v[j], b_w_in[j], b_lambda[j], b_subln[j],
                                            lambda_init)
            w_out = b_w_out[j].astype(BF16)
            b_new_p.append(new_p)
            b_new_s.append(new_s)
        yp = _post_mixer(yp, mp, w_out, ln_gain[i, 0], ln_bias[i, 0])
        ys = _post_mixer(ys, ms, w_out, ln_gain[i, 0], ln_bias[i, 0])
        w13 = ffn_w13[i].astype(BF16)
        w1, w3 = w13[:, :d_ff], w13[:, d_ff:]
        w2 = ffn_w2[i].astype(BF16)
        wg = ple_w_gate[i].astype(BF16)
        wp = ple_w_proj[i].astype(BF16)
        yp = _channel_and_ple(yp, p_prompt[i].reshape(bp * s, -1), w1, w3, w2, wg, wp, ln_gain[i, 1], ln_bias[i, 1])
        ys = _channel_and_ple(ys, p_sample[i].reshape(bs * t, -1), w1, w3, w2, wg, wp, ln_gain[i, 1], ln_bias[i, 1])

    def stack(items, idx):
        return jnp.stack([it[idx] for it in items])

    return (yp.reshape(bp, s, d), ys.reshape(bs, t, d),
            stack(a_new_p, 0), stack(a_new_p, 1), stack(a_new_p, 2),
            stack(b_new_p, 0), stack(b_new_p, 1),
            stack(a_new_s, 0), stack(a_new_s, 1), stack(a_new_s, 2),
            stack(b_new_s, 0), stack(b_new_s, 1))
```

```python
import functools
import math

import jax
import jax.numpy as jnp
import ml_dtypes
import numpy as np
from jax import lax
from jax.experimental import pallas as pl
from jax.experimental.pallas import tpu as pltpu

F32 = jnp.float32
BF16 = jnp.bfloat16
I32 = jnp.int32

CHUNK = 64
CHUNK_SHIFT = 6
A_HEADS = 16
A_HEAD_DIM = 64
A_KV_HEADS = 4
A_GROUP = A_HEADS // A_KV_HEADS
IDX_HEADS = 8
IDX_DIM = 64
TOPK_MAX = 256
IDX_SCALE = IDX_DIM ** -0.5 * IDX_HEADS ** -0.5
B_HEADS = 8
B_HEAD_DIM = 64
DEPTH = 2
ALPHA = (2 * DEPTH) ** 0.25
LN_EPS = 1e-5

LANES = 128
HALF = LANES // 2
VMEM_LIMIT_BYTES = 56 * 1024 * 1024

NEG_FILL = -1e30
M_INIT = -5e29
INT_MIN = -(2 ** 31)
KEY_NEG_INF = int(np.int32(np.uint32(0xFF800000) ^ np.uint32(0x7FFFFFFF)))
POS_BIG = 2 ** 30

B_ROW_GROUP = 256
COUNT_ROWS = 64
BITS_PER_CHECK = 4
UNCHECKED_GROUPS = 5
LOG2E = float(np.log2(np.e))
NT_DIMS = (((1,), (1,)), ((), ()))


def _alibi_slopes(n_heads):
    return [float(2.0 ** (-8.0 * (h + 1) / n_heads)) for h in range(n_heads)]


def _compiler_params(semantics):
    return pltpu.CompilerParams(dimension_semantics=semantics,
                                vmem_limit_bytes=VMEM_LIMIT_BYTES)


def _row_tile(m, want):
    return want if m % want == 0 else m


def _block_geometry(pos0, i, tq, tk, l_valid):
    q0 = pos0 + i * tq
    qpos = q0 + lax.broadcasted_iota(I32, (tq, 1), 0)
    limit = jnp.minimum((lax.shift_right_logical(qpos, CHUNK_SHIFT) + 1) * CHUNK, l_valid)
    lim_max = jnp.minimum((lax.shift_right_logical(q0 + tq - 1, CHUNK_SHIFT) + 1) * CHUNK, l_valid)
    return qpos, limit, (lim_max + tk - 1) // tk


def _proj_kernel(x_ref, *refs, out_plan):
    n_w = len(out_plan)
    w_refs, o_refs = refs[:n_w], refs[n_w:]
    x = x_ref[...].astype(BF16)
    oi = 0
    for w_ref, outs in zip(w_refs, out_plan):
        y = jnp.dot(x, w_ref[...], preferred_element_type=F32)
        for dtype, scale in outs:
            o_refs[oi][...] = (y if scale == 1.0 else y * scale).astype(dtype)
            oi += 1


def _project(x, weights, out_plan, tm=512):
    m, k = x.shape
    tm = _row_tile(m, tm)
    in_specs = [pl.BlockSpec((tm, k), lambda i: (i, 0))]
    out_shape, out_specs = [], []
    for w, outs in zip(weights, out_plan):
        n = w.shape[1]
        in_specs.append(pl.BlockSpec((k, n), lambda i: (0, 0)))
        for dtype, _ in outs:
            out_shape.append(jax.ShapeDtypeStruct((m, n), dtype))
            out_specs.append(pl.BlockSpec((tm, n), lambda i: (i, 0)))
    return pl.pallas_call(
        functools.partial(_proj_kernel, out_plan=out_plan),
        grid=(m // tm,),
        in_specs=in_specs,
        out_specs=out_specs,
        out_shape=out_shape,
        compiler_params=_compiler_params(("parallel",)),
        name="proj",
    )(x, *weights)


def _row_fold(x, rows):
    part = x[0:rows]
    for s in range(1, x.shape[0] // rows):
        part = part + x[s * rows:(s + 1) * rows]
    return part


def _split_bf16(x):
    hi = float(np.float32(x).astype(ml_dtypes.bfloat16))
    return hi, x - hi


def _attn_a_kernel(q_ref, qi_ref, wq_ref, kiwi_ref, k_ref, v_ref, o_ref,
                   key_ref, pf_ref, qs_ref, qis_ref, m_ref, acc_ref,
                   *, tq, tk, pos0, l_valid, t_valid, n_sel, pos_bits):
    i = pl.program_id(1)
    nslab = tk // LANES
    qpos, limit, nk = _block_geometry(pos0, i, tq, tk, l_valid)
    lane = lax.broadcasted_iota(I32, (1, LANES), 1)
    qpos_l = pos0 + i * tq + lax.broadcasted_iota(I32, (1, tq), 1)
    limit_l = jnp.minimum((lax.shift_right_logical(qpos_l, CHUNK_SHIFT) + 1) * CHUNK, l_valid)
    row_ok = lax.broadcasted_iota(I32, (1, tq), 1) < t_valid
    slopes2 = [s * LOG2E for s in _alibi_slopes(A_HEADS)]

    eye = jnp.where(lax.broadcasted_iota(I32, (tq, LANES), 0) == lax.broadcasted_iota(I32, (tq, LANES), 1),
                    1.0, 0.0).astype(BF16)
    lane_q = lax.broadcasted_iota(I32, (tq, LANES), 1)
    for c in range(A_KV_HEADS):
        fb = (1 - c % 2) * HALF
        for g in range(A_GROUP):
            h = c * A_GROUP + g
            s_hi, s_lo = _split_bf16(slopes2[h])
            coef = jnp.where(lane_q == fb, 32.0 * s_hi,
                             jnp.where(lane_q == fb + 1, s_hi,
                                       jnp.where(lane_q == fb + 2, 32.0 * s_lo, s_lo))).astype(BF16)
            is_coef = (lane_q >= fb) & (lane_q < fb + 4)
            rows = slice(g * tq, (g + 1) * tq)
            qs_ref[c, rows, 0:LANES] = jnp.where(is_coef, coef, q_ref[0, :, h * LANES:(h + 1) * LANES])
            qs_ref[c, rows, LANES:2 * LANES] = eye
    for h in range(IDX_HEADS):
        qis_ref[h * tq:(h + 1) * tq, :] = qi_ref[0, :, h * LANES:(h + 1) * LANES]
    krow = lax.broadcasted_iota(I32, (2 * tk, LANES), 0)
    r_hi = lax.shift_right_logical(krow, 5).astype(F32)
    r_lo = (krow & 31).astype(F32)
    lane_2k = lax.broadcasted_iota(I32, (2 * tk, LANES), 1)
    lane_k = lax.broadcasted_iota(I32, (tk, LANES), 1)
    for par in range(2):
        fb = (1 - par) * HALF
        pf_ref[par] = jnp.where((lane_2k == fb) | (lane_2k == fb + 2), r_hi,
                                jnp.where((lane_2k == fb + 1) | (lane_2k == fb + 3), r_lo, 0.0)).astype(BF16)

    w_t = jnp.transpose(wq_ref[0])[IDX_DIM:IDX_DIM + IDX_HEADS] * IDX_SCALE

    half_rows = IDX_HEADS // 2 * tq

    def score_body(j, carry):
        kic = kiwi_ref[0, j]
        ds = [lax.dot_general(kic, qis_ref[r * half_rows:(r + 1) * half_rows, :], NT_DIMS,
                              preferred_element_type=F32) for r in range(2)]
        sc = None
        for h in range(IDX_HEADS):
            d = ds[h // 4][:, (h % 4) * tq:(h % 4 + 1) * tq]
            term = jnp.maximum(d, 0.0) * w_t[h:h + 1]
            sc = term if sc is None else sc + term
        kpos = j * tk + lax.broadcasted_iota(I32, (tk, tq), 0)
        sc = jnp.where(kpos < limit_l, sc, -jnp.inf)
        bits = lax.bitcast_convert_type(sc, I32)
        key = bits ^ (lax.shift_right_arithmetic(bits, 31) & 0x7FFFFFFF)
        key_ref[j] = jnp.where(bits == INT_MIN, 0, key)
        return carry

    lax.fori_loop(0, nk, score_body, 0)

    def count(pred):
        def body(j, acc):
            kpos = j * tk + lax.broadcasted_iota(I32, (tk, tq), 0)
            return acc + _row_fold(jnp.where(pred(key_ref[j], kpos), 1.0, 0.0), COUNT_ROWS)
        acc = lax.fori_loop(0, nk, body, jnp.zeros((COUNT_ROWS, tq), F32))
        return jnp.sum(acc, axis=0, keepdims=True)

    k_f = float(n_sel)
    n_all = (jnp.zeros((1, tq), I32) + nk * tk).astype(F32)
    c_nonneg = count(lambda u, kp: u >= 0)
    take = c_nonneg >= k_f
    thr0 = jnp.where(take, 0, INT_MIN).astype(I32)
    c_ge0 = jnp.where(take, c_nonneg, n_all)

    def unsettled(c_ge):
        return jnp.sum(jnp.where((c_ge != k_f) & row_ok, 1.0, 0.0)) > 0.0

    def bits_cond(carry):
        grp, _, c_ge = carry
        return (grp * BITS_PER_CHECK < 31) & unsettled(c_ge)

    def resolve_bits(grp, thr, c_ge):
        for bb in range(BITS_PER_CHECK):
            b = grp * BITS_PER_CHECK + bb
            bit = jnp.where(b <= 30, lax.shift_left(jnp.int32(1), 30 - jnp.minimum(b, 30)), 0)
            cand = thr | bit
            c = count(lambda u, kp: u >= cand)
            take = c >= k_f
            thr, c_ge = jnp.where(take, cand, thr), jnp.where(take, c, c_ge)
        return thr, c_ge

    thr, c_ge = lax.fori_loop(0, UNCHECKED_GROUPS, lambda grp, c: resolve_bits(grp, *c), (thr0, c_ge0))
    _, thr, c_ge = lax.while_loop(bits_cond, lambda c: (c[0] + 1,) + resolve_bits(*c),
                                  (jnp.int32(UNCHECKED_GROUPS), thr, c_ge))

    surplus = jnp.where((c_ge > k_f) & (thr != KEY_NEG_INF) & row_ok, 1.0, 0.0)
    any_surplus = jnp.sum(surplus) > 0.0

    def tie_cut():
        need = k_f - count(lambda u, kp: u > thr)

        def body(b, x):
            cand = x | lax.shift_left(jnp.int32(1), pos_bits - 1 - b)
            c = count(lambda u, kp: (u == thr) & (kp < cand))
            return jnp.where(c < need, cand, x)
        return lax.fori_loop(0, pos_bits, body, jnp.zeros((1, tq), I32))

    cut = lax.cond(any_surplus, tie_cut, lambda: jnp.full((1, tq), POS_BIG, I32))

    m_ref[...] = jnp.full(m_ref.shape, M_INIT, F32)
    acc_ref[...] = jnp.zeros(acc_ref.shape, F32)

    def process_chunks(j0, n, diag):
        base = jnp.full((1, LANES), j0 * tk, I32).astype(F32)
        masks, k_alls, v_alls = [], [], []
        for d in range(n):
            u = key_ref[j0 + d]
            kpos = (j0 + d) * tk + lax.broadcasted_iota(I32, (tk, tq), 0)
            sel = (u > thr) | ((u == thr) & (kpos <= cut))
            if diag and d == n - 1:
                sel = sel & (kpos < limit_l)
            masks.append(jnp.where(sel, 0.0, NEG_FILL).astype(BF16))
            k_alls.append(k_ref[0, j0 + d])
            v_alls.append(v_ref[0, j0 + d])
        rhs, vvs = [], []
        for c in range(A_KV_HEADS):
            sl = slice((c // 2) * LANES, (c // 2 + 1) * LANES)
            keep = (lane_k < HALF) if c % 2 == 0 else (lane_k >= HALF)
            rhs.append(jnp.concatenate(
                [jnp.concatenate([jnp.where(keep, k_alls[d][:, sl], pf_ref[c % 2, d * tk:(d + 1) * tk]),
                                  masks[d]], axis=1) for d in range(n)], axis=0))
            vvs.append(jnp.concatenate(
                [jnp.where(keep, v_alls[d][:, sl], jnp.ones_like(v_alls[d][:, sl])) for d in range(n)],
                axis=0))
        ss = [lax.dot_general(qs_ref[c], rhs[c], NT_DIMS, preferred_element_type=F32)
              for c in range(A_KV_HEADS)]

        slabs = range(n * nslab)
        if diag:
            qrel = qpos - j0 * tk
            future = {sb: 2.0 * jnp.maximum((lane + sb * LANES) - qrel, 0).astype(F32)
                      for sb in slabs if sb >= (n - 1) * nslab}

        for c in range(A_KV_HEADS):
            ps, alphas = [], []
            for g in range(A_GROUP):
                slope = slopes2[c * A_GROUP + g]
                rows = slice(g * tq, (g + 1) * tq)
                ts = [ss[c][rows, sb * LANES:(sb + 1) * LANES] for sb in slabs]
                if diag:
                    ts = [t - slope * future[sb] if sb in future else t for sb, t in zip(slabs, ts)]
                c_j = slope * base
                mx = ts[0]
                for t in ts[1:]:
                    mx = jnp.maximum(mx, t)
                m_old = m_ref[c, rows, :]
                m_new = jnp.maximum(m_old, jnp.max(mx, axis=1, keepdims=True) + c_j)
                ref = m_new - c_j
                alphas.append(jnp.exp2(m_old - m_new))
                ps.append(jnp.concatenate([jnp.exp2(t - ref).astype(BF16) for t in ts], axis=1))
                m_ref[c, rows, :] = m_new
            pv = jnp.dot(jnp.concatenate(ps, axis=0), vvs[c], preferred_element_type=F32)
            acc_ref[c] = acc_ref[c] * jnp.concatenate(alphas, axis=0) + pv

    n_past = nk - 1

    def past_pair(jj, carry):
        process_chunks(2 * jj, 2, diag=False)
        return carry

    lax.fori_loop(0, n_past // 2, past_pair, 0)

    @pl.when(n_past % 2 == 1)
    def _():
        process_chunks(nk - 2, 2, diag=True)

    @pl.when(n_past % 2 == 0)
    def _():
        process_chunks(nk - 1, 1, diag=True)

    for c in range(A_KV_HEADS):
        for g in range(A_GROUP):
            h = c * A_GROUP + g
            a = acc_ref[c, g * tq:(g + 1) * tq, :]
            val = a / pltpu.roll(a, HALF, axis=1)
            off = (c % 2) * HALF
            o_ref[0, :, h * A_HEAD_DIM:(h + 1) * A_HEAD_DIM] = val[:, off:off + HALF].astype(o_ref.dtype)


def _attend_a(q, qi, kiwi_q, kiwi_k, k, v, *, tk, pos0, l_valid, n_sel):
    b, t, _ = q.shape
    tq = LANES
    t_pad = -(-t // tq) * tq
    if t_pad != t:
        q, qi, kiwi_q = (jnp.pad(a, ((0, 0), (0, t_pad - t), (0, 0))) for a in (q, qi, kiwi_q))
    l = k.shape[1]
    nkc = l // tk
    assert tk % tq == 0 and pos0 % tq == 0 and l % tk == 0 and 2 * tk <= 32 * 256 and pos0 + t_pad <= l
    pos_bits = max(1, int(math.ceil(math.log2(l + 1))))
    kvc = A_KV_HEADS * A_HEAD_DIM
    kiwi_k = kiwi_k.reshape(b, nkc, tk, LANES)
    k = k.reshape(b, nkc, tk, kvc)
    v = v.reshape(b, nkc, tk, kvc)
    kern = functools.partial(_attn_a_kernel, tq=tq, tk=tk, pos0=pos0, l_valid=l_valid, t_valid=t,
                             n_sel=n_sel, pos_bits=pos_bits)
    ocols = A_HEADS * A_HEAD_DIM
    out = pl.pallas_call(
        kern,
        grid=(b, t_pad // tq),
        in_specs=[
            pl.BlockSpec((1, tq, A_HEADS * LANES), lambda bi, i: (bi, i, 0)),
            pl.BlockSpec((1, tq, IDX_HEADS * LANES), lambda bi, i: (bi, i, 0)),
            pl.BlockSpec((1, tq, LANES), lambda bi, i: (bi, i, 0)),
            pl.BlockSpec((1, nkc, tk, LANES), lambda bi, i: (bi, 0, 0, 0)),
            pl.BlockSpec((1, nkc, tk, kvc), lambda bi, i: (bi, 0, 0, 0)),
            pl.BlockSpec((1, nkc, tk, kvc), lambda bi, i: (bi, 0, 0, 0)),
        ],
        out_specs=pl.BlockSpec((1, tq, ocols), lambda bi, i: (bi, i, 0)),
        out_shape=jax.ShapeDtypeStruct((b, t_pad, ocols), BF16),
        scratch_shapes=[
            pltpu.VMEM((nkc, tk, tq), I32),
            pltpu.VMEM((2, 2 * tk, LANES), BF16),
            pltpu.VMEM((A_KV_HEADS, A_GROUP * tq, 2 * LANES), BF16),
            pltpu.VMEM((IDX_HEADS * tq, LANES), BF16),
            pltpu.VMEM((A_KV_HEADS, A_GROUP * tq, LANES), F32),
            pltpu.VMEM((A_KV_HEADS, A_GROUP * tq, LANES), F32),
        ],
        compiler_params=_compiler_params(("parallel", "arbitrary")),
        name="attn_a",
    )(q, qi, kiwi_q, kiwi_k, k, v)
    return out[:, :t]


def _attn_b_kernel(q_ref, k_ref, v_ref, lam_ref, slope_ref, g_ref, o_ref,
                   qq_ref, m_ref, acc_ref, *, tq, tk, pos0, l_valid, lambda_init):
    h = pl.program_id(1)
    i = pl.program_id(2)
    nslab = tk // LANES
    qpos, limit, nk = _block_geometry(pos0, i, tq, tk, l_valid)
    slope2 = slope_ref[pl.ds(h, 1), :] * LOG2E
    lane = lax.broadcasted_iota(I32, (1, LANES), 1)

    q = q_ref[0]
    lane_q = lax.broadcasted_iota(I32, (tq, LANES), 1)
    zero = jnp.zeros_like(q)
    qq_ref[0:tq, :] = jnp.where(lane_q < B_HEAD_DIM, q, zero)
    qq_ref[tq:2 * tq, :] = jnp.where(lane_q >= B_HEAD_DIM, q, zero)

    m_ref[...] = jnp.full(m_ref.shape, M_INIT, F32)
    acc_ref[...] = jnp.zeros(acc_ref.shape, F32)

    def finish_rows(j, rs, ts, vv):
        c_j = slope2 * (j * tk).astype(F32)
        mx = ts[0]
        for t in ts[1:]:
            mx = jnp.maximum(mx, t)
        m_old = m_ref[rs, :]
        m_new = jnp.maximum(m_old, jnp.max(mx, axis=1, keepdims=True) + c_j)
        ref = m_new - c_j
        alpha = jnp.exp2(m_old - m_new)
        p = jnp.concatenate([jnp.exp2(t - ref).astype(BF16) for t in ts], axis=1)
        pv = jnp.dot(p, vv, preferred_element_type=F32)
        acc_ref[rs, :] = acc_ref[rs, :] * jnp.concatenate([alpha, alpha], axis=1) + pv
        m_ref[rs, :] = m_new

    def chunk_operands(j):
        vc = v_ref[0, j]
        return k_ref[0, j], jnp.concatenate([vc, jnp.ones_like(vc)], axis=1)

    rg = min(tq, B_ROW_GROUP)
    groups = [slice(r, r + rg) for r in range(0, 2 * tq, rg)]

    def visible_chunk(j, carry):
        kc, vv = chunk_operands(j)
        ss = [lax.dot_general(qq_ref[rs, :], kc, NT_DIMS, preferred_element_type=F32) for rs in groups]
        for rs, s in zip(groups, ss):
            ts = []
            for g in range(nslab):
                kb = slope2 * (lane + g * LANES).astype(F32)
                ts.append(s[:, g * LANES:(g + 1) * LANES] + kb)
            finish_rows(j, rs, ts, vv)
        return carry

    lax.fori_loop(0, nk - 1, visible_chunk, 0)

    j = nk - 1
    kc, vv = chunk_operands(j)
    qrel = qpos - j * tk
    qrel_f = qrel.astype(F32)
    lim_rel = limit - j * tk
    block_starts_chunk = tq == tk and pos0 % tk == 0
    plans = []
    for rs in groups:
        q_lo = rs.start % tq
        n_vis = (q_lo + rg) // LANES if block_starts_chunk else nslab
        n_past = q_lo // LANES if block_starts_chunk else 0
        plans.append((rs, slice(q_lo, q_lo + rg), n_vis, n_past))
    ss = [lax.dot_general(qq_ref[rs, :], kc[0:n_vis * LANES], NT_DIMS, preferred_element_type=F32)
          for rs, _, n_vis, _ in plans]
    for (rs, qr, n_vis, n_past), s in zip(plans, ss):
        ts = []
        for g in range(n_vis):
            krel = lane + g * LANES
            sg = s[:, g * LANES:(g + 1) * LANES]
            if g < n_past:
                ts.append(sg + slope2 * krel.astype(F32))
            else:
                bias = slope2 * (jnp.abs((qrel[qr] - krel).astype(F32)) - qrel_f[qr])
                ts.append(jnp.where(krel < lim_rel[qr], sg - bias, NEG_FILL))
        finish_rows(j, rs, ts, vv[0:n_vis * LANES])

    hd = 2 * B_HEAD_DIM
    lp = lam_ref[...]
    lam = (jnp.exp(jnp.sum(lp[0:1] * lp[1:2], axis=1, keepdims=True))
           - jnp.exp(jnp.sum(lp[2:3] * lp[3:4], axis=1, keepdims=True)) + lambda_init)
    out = (acc_ref[0:tq, 0:hd] / acc_ref[0:tq, hd:2 * hd]
           - lam * (acc_ref[tq:2 * tq, 0:hd] / acc_ref[tq:2 * tq, hd:2 * hd]))
    rms = lax.rsqrt(jnp.mean(out * out, axis=1, keepdims=True) + LN_EPS)
    o_ref[0] = (out * rms * g_ref[...] * (1.0 - lambda_init)).astype(o_ref.dtype)


def _attend_b(q, k, v, lam_params, subln_g, *, tq, tk, pos0, l_valid, lambda_init):
    b, t, cols = q.shape
    l = k.shape[1]
    nkc = l // tk
    hd = 2 * B_HEAD_DIM
    assert tk % tq == 0 and pos0 % tq == 0 and t % tq == 0 and l % tk == 0
    k = k.reshape(b, nkc, tk, cols)
    v = v.reshape(b, nkc, tk, cols)
    slopes = jnp.broadcast_to(jnp.asarray(_alibi_slopes(B_HEADS), F32)[:, None], (B_HEADS, LANES))
    kern = functools.partial(_attn_b_kernel, tq=tq, tk=tk, pos0=pos0, l_valid=l_valid,
                             lambda_init=lambda_init)
    return pl.pallas_call(
        kern,
        grid=(b, B_HEADS, t // tq),
        in_specs=[
            pl.BlockSpec((1, tq, hd), lambda bi, h, i: (bi, i, h)),
            pl.BlockSpec((1, nkc, tk, hd), lambda bi, h, i: (bi, 0, 0, h)),
            pl.BlockSpec((1, nkc, tk, hd), lambda bi, h, i: (bi, 0, 0, h)),
            pl.BlockSpec((4, B_HEAD_DIM), lambda bi, h, i: (0, 0)),
            pl.BlockSpec((B_HEADS, LANES), lambda bi, h, i: (0, 0)),
            pl.BlockSpec((1, hd), lambda bi, h, i: (0, 0)),
        ],
        out_specs=pl.BlockSpec((1, tq, hd), lambda bi, h, i: (bi, i, h)),
        out_shape=jax.ShapeDtypeStruct((b, t, cols), BF16),
        scratch_shapes=[
            pltpu.VMEM((2 * tq, hd), BF16),
            pltpu.VMEM((2 * tq, LANES), F32),
            pltpu.VMEM((2 * tq, 2 * hd), F32),
        ],
        compiler_params=_compiler_params(("parallel", "parallel", "arbitrary")),
        name="attn_b",
    )(q, k, v, lam_params, slopes, subln_g.reshape(1, hd))


def _layer_norm(z, g, b):
    mu = jnp.mean(z, axis=1, keepdims=True)
    zc = z - mu
    var = jnp.mean(zc * zc, axis=1, keepdims=True)
    return zc * lax.rsqrt(var + LN_EPS) * g + b


def _post_kernel(x_ref, a_ref, w_ref, g_ref, b_ref, o_ref):
    mix = jnp.dot(a_ref[...], w_ref[...], preferred_element_type=F32)
    o_ref[...] = _layer_norm(ALPHA * x_ref[...] + mix, g_ref[...], b_ref[...])


def _post_mixer(x, a, w_out, g, b, tm=512):
    m, d = x.shape
    tm = _row_tile(m, tm)
    return pl.pallas_call(
        _post_kernel,
        grid=(m // tm,),
        in_specs=[
            pl.BlockSpec((tm, d), lambda i: (i, 0)),
            pl.BlockSpec((tm, a.shape[1]), lambda i: (i, 0)),
            pl.BlockSpec(w_out.shape, lambda i: (0, 0)),
            pl.BlockSpec((1, d), lambda i: (0, 0)),
            pl.BlockSpec((1, d), lambda i: (0, 0)),
        ],
        out_specs=pl.BlockSpec((tm, d), lambda i: (i, 0)),
        out_shape=jax.ShapeDtypeStruct((m, d), F32),
        compiler_params=_compiler_params(("parallel",)),
        name="post_mixer",
    )(x, a, w_out, g.reshape(1, d), b.reshape(1, d))


def _sigmoid(x):
    return 1.0 / (1.0 + jnp.exp(-x))


def _ffn_kernel(x_ref, p_ref, w1_ref, w3_ref, w2_ref, wg_ref, wp_ref, g_ref, b_ref, o_ref, acc_ref):
    f = pl.program_id(1)
    x = x_ref[...]
    xb = x.astype(BF16)

    @pl.when(f == 0)
    def _():
        gate = _sigmoid(jnp.dot(xb, wg_ref[...], preferred_element_type=F32))
        ple = jnp.dot(p_ref[...].astype(BF16), wp_ref[...], preferred_element_type=F32)
        acc_ref[...] = ALPHA * x + gate * ple

    gt = jnp.dot(xb, w1_ref[...], preferred_element_type=F32)
    up = jnp.dot(xb, w3_ref[...], preferred_element_type=F32)
    hid = (gt * _sigmoid(gt)) * up
    acc_ref[...] += jnp.dot(hid.astype(BF16), w2_ref[...], preferred_element_type=F32)

    @pl.when(f == pl.num_programs(1) - 1)
    def _():
        o_ref[...] = _layer_norm(acc_ref[...], g_ref[...], b_ref[...])


def _ffn_tile(d_ff):
    for nf in (1, 2, 3, 4, 5, 6, 7, 8, 11, 22):
        if d_ff % nf == 0 and (d_ff // nf) % LANES == 0 and d_ff // nf <= 1536:
            return d_ff // nf
    return d_ff


def _channel_and_ple(x, p, w1, w3, w2, wg, wp, g, b, tm=512):
    m, d = x.shape
    tm = _row_tile(m, tm)
    d_ff = w1.shape[1]
    tf = _ffn_tile(d_ff)
    return pl.pallas_call(
        _ffn_kernel,
        grid=(m // tm, d_ff // tf),
        in_specs=[
            pl.BlockSpec((tm, d), lambda i, f: (i, 0)),
            pl.BlockSpec((tm, p.shape[1]), lambda i, f: (i, 0)),
            pl.BlockSpec((d, tf), lambda i, f: (0, f)),
            pl.BlockSpec((d, tf), lambda i, f: (0, f)),
            pl.BlockSpec((tf, d), lambda i, f: (f, 0)),
            pl.BlockSpec(wg.shape, lambda i, f: (0, 0)),
            pl.BlockSpec(wp.shape, lambda i, f: (0, 0)),
            pl.BlockSpec((1, d), lambda i, f: (0, 0)),
            pl.BlockSpec((1, d), lambda i, f: (0, 0)),
        ],
        out_specs=pl.BlockSpec((tm, d), lambda i, f: (i, 0)),
        out_shape=jax.ShapeDtypeStruct((m, d), F32),
        scratch_shapes=[pltpu.VMEM((tm, d), F32)],
        compiler_params=_compiler_params(("parallel", "arbitrary")),
        name="ffn_ple",
    )(x, p, w1, w3, w2, wg, wp, g.reshape(1, d), b.reshape(1, d))


def _pad_keys(x, mult):
    l = x.shape[1]
    lp = -(-l // mult) * mult
    if lp == l:
        return x
    return jnp.pad(x, ((0, 0), (0, lp - l)) + ((0, 0),) * (x.ndim - 2))


def _head_slabs(w, n_heads, dim, high_half):
    k = w.shape[0]
    w3 = w.reshape(k, n_heads, dim)
    low = jnp.pad(w3, ((0, 0), (0, 0), (0, LANES - dim)))
    high = jnp.pad(w3, ((0, 0), (0, 0), (LANES - dim, 0)))
    sel = jnp.asarray(high_half, bool)[None, :, None]
    return jnp.where(sel, high, low).reshape(k, n_heads * LANES)


def _split_a_weights(w_in):
    qc = A_HEADS * A_HEAD_DIM
    kvc = A_KV_HEADS * A_HEAD_DIM
    ic = IDX_HEADS * IDX_DIM
    w_q, w_k, w_v = w_in[:, :qc], w_in[:, qc:qc + kvc], w_in[:, qc + kvc:qc + 2 * kvc]
    o = qc + 2 * kvc
    w_qi = w_in[:, o:o + ic]
    tail = w_in[:, o + ic:o + ic + IDX_DIM + IDX_HEADS]
    tail = jnp.pad(tail, ((0, 0), (0, LANES - tail.shape[1])))
    q_high = [(h // A_GROUP) % 2 == 1 for h in range(A_HEADS)]
    pieces = [_head_slabs(w_q, A_HEADS, A_HEAD_DIM, q_high), w_k, w_v,
              _head_slabs(w_qi, IDX_HEADS, IDX_DIM, [False] * IDX_HEADS), tail]
    return [p.astype(BF16) for p in pieces]


def _project_a(x2d, w_pieces):
    plan = [[(BF16, A_HEAD_DIM ** -0.5 * LOG2E)],
            [(F32, 1.0), (BF16, 1.0)],
            [(F32, 1.0), (BF16, 1.0)],
            [(BF16, 1.0)],
            [(F32, 1.0), (BF16, 1.0)]]
    return _project(x2d, w_pieces, plan)


def _mixer_a(xp, xs, cache_k, cache_v, cache_ki, w_in):
    bp, s, d = xp.shape
    bs, t, _ = xs.shape
    past = cache_k.shape[1]
    kvc = A_KV_HEADS * A_HEAD_DIM
    w_pieces = _split_a_weights(w_in)

    q, k32, k16, v32, v16, qi, kiwi32, kiwi16 = _project_a(xp.reshape(bp * s, d), w_pieces)
    n_sel_p = min(TOPK_MAX, s // 4)
    tk_p = 512 if s % 512 == 0 else s
    att_p = _attend_a(q.reshape(bp, s, -1), qi.reshape(bp, s, -1), kiwi32.reshape(bp, s, LANES),
                      kiwi16.reshape(bp, s, LANES), k16.reshape(bp, s, kvc), v16.reshape(bp, s, kvc),
                      tk=tk_p, pos0=0, l_valid=s, n_sel=n_sel_p)
    new_p = (k32.reshape(bp, s, A_KV_HEADS, A_HEAD_DIM), v32.reshape(bp, s, A_KV_HEADS, A_HEAD_DIM),
             kiwi32.reshape(bp, s, LANES)[:, :, :IDX_DIM])

    qs, k32s, k16s, v32s, v16s, qis, kiwi32s, kiwi16s = _project_a(xs.reshape(bs * t, d), w_pieces)
    l_s = past + t
    n_sel_s = min(TOPK_MAX, l_s // 4)
    tk_s = 384
    k_all = _pad_keys(jnp.concatenate([cache_k.reshape(bs, past, kvc).astype(BF16),
                                       k16s.reshape(bs, t, kvc)], axis=1), tk_s)
    v_all = _pad_keys(jnp.concatenate([cache_v.reshape(bs, past, kvc).astype(BF16),
                                       v16s.reshape(bs, t, kvc)], axis=1), tk_s)
    ki_cache = jnp.pad(cache_ki.astype(BF16), ((0, 0), (0, 0), (0, LANES - IDX_DIM)))
    ki_all = _pad_keys(jnp.concatenate([ki_cache, kiwi16s.reshape(bs, t, LANES)], axis=1), tk_s)
    att_s = _attend_a(qs.reshape(bs, t, -1), qis.reshape(bs, t, -1), kiwi32s.reshape(bs, t, LANES),
                      ki_all, k_all, v_all, tk=tk_s, pos0=past, l_valid=l_s, n_sel=n_sel_s)
    new_s = (k32s.reshape(bs, t, A_KV_HEADS, A_HEAD_DIM), v32s.reshape(bs, t, A_KV_HEADS, A_HEAD_DIM),
             kiwi32s.reshape(bs, t, LANES)[:, :, :IDX_DIM])
    return att_p.reshape(bp * s, -1), att_s.reshape(bs * t, -1), new_p, new_s


def _project_b(x2d, w_in_bf):
    cols = B_HEADS * 2 * B_HEAD_DIM
    w_pieces = [w_in_bf[:, :cols], w_in_bf[:, cols:2 * cols], w_in_bf[:, 2 * cols:]]
    plan = [[(BF16, B_HEAD_DIM ** -0.5 * LOG2E)],
            [(F32, 1.0), (BF16, 1.0)],
            [(F32, 1.0), (BF16, 1.0)]]
    return _project(x2d, w_pieces, plan)


def _mixer_b(xp, xs, cache_k, cache_v, w_in, lam_params, subln_g, lambda_init):
    bp, s, d = xp.shape
    bs, t, _ = xs.shape
    past = cache_k.shape[1]
    cols = B_HEADS * 2 * B_HEAD_DIM
    w_in_bf = w_in.astype(BF16)

    q, k32, k16, v32, v16 = _project_b(xp.reshape(bp * s, d), w_in_bf)
    tq_p = 1024 if s % 1024 == 0 else s
    tk_p = 1024 if s % 1024 == 0 else s
    att_p = _attend_b(q.reshape(bp, s, cols), k16.reshape(bp, s, cols), v16.reshape(bp, s, cols),
                      lam_params, subln_g, tq=tq_p, tk=tk_p, pos0=0, l_valid=s, lambda_init=lambda_init)
    new_p = (k32.reshape(bp, s, B_HEADS, 2 * B_HEAD_DIM), v32.reshape(bp, s, B_HEADS, 2 * B_HEAD_DIM))

    qs, k32s, k16s, v32s, v16s = _project_b(xs.reshape(bs * t, d), w_in_bf)
    l_s = past + t
    tk_s = 384
    k_all = _pad_keys(jnp.concatenate([cache_k.reshape(bs, past, cols).astype(BF16),
                                       k16s.reshape(bs, t, cols)], axis=1), tk_s)
    v_all = _pad_keys(jnp.concatenate([cache_v.reshape(bs, past, cols).astype(BF16),
                                       v16s.reshape(bs, t, cols)], axis=1), tk_s)
    att_s = _attend_b(qs.reshape(bs, t, cols), k_all, v_all, lam_params, subln_g,
                      tq=t, tk=tk_s, pos0=past, l_valid=l_s, lambda_init=lambda_init)
    new_s = (k32s.reshape(bs, t, B_HEADS, 2 * B_HEAD_DIM), v32s.reshape(bs, t, B_HEADS, 2 * B_HEAD_DIM))
    return att_p.reshape(bp * s, cols), att_s.reshape(bs * t, cols), new_p, new_s


def kernel(x_prompt, x_sample, cache_a_k, cache_a_v, cache_a_kidx, cache_b_k, cache_b_v, p_prompt, p_sample, a_w_in, a_w_out, b_w_in, b_lambda, b_subln, b_w_out, ffn_w13, ffn_w2, ple_w_proj, ple_w_gate, ln_gain, ln_bias):
    bp, s, d = x_prompt.shape
    bs, t, _ = x_sample.shape
    depth = ffn_w13.shape[0]
    d_ff = ffn_w2.shape[1]

    yp = x_prompt.reshape(bp * s, d)
    ys = x_sample.reshape(bs * t, d)
    a_new_p, a_new_s, b_new_p, b_new_s = [], [], [], []
    for i in range(depth):
        j = i // 2
        if i % 2 == 0:
            mp, ms, new_p, new_s = _mixer_a(yp.reshape(bp, s, d), ys.reshape(bs, t, d),
                                            cache_a_k[j], cache_a_v[j], cache_a_kidx[j], a_w_in[j])
            w_out = a_w_out[j].astype(BF16)
            a_new_p.append(new_p)
            a_new_s.append(new_s)
        else:
            lambda_init = 0.8 - 0.6 * math.exp(-0.3 * i)
            mp, ms, new_p, new_s = _mixer_b(yp.reshape(bp, s, d), ys.reshape(bs, t, d),
                                            cache_b_k[j], cache_b_v[j], b_w_in[j], b_lambda[j], b_subln[j],
                                            lambda_init)
            w_out = b_w_out[j].astype(BF16)
            b_new_p.append(new_p)
            b_new_s.append(new_s)
        yp = _post_mixer(yp, mp, w_out, ln_gain[i, 0], ln_bias[i, 0])
        ys = _post_mixer(ys, ms, w_out, ln_gain[i, 0], ln_bias[i, 0])
        w13 = ffn_w13[i].astype(BF16)
        w1, w3 = w13[:, :d_ff], w13[:, d_ff:]
        w2 = ffn_w2[i].astype(BF16)
        wg = ple_w_gate[i].astype(BF16)
        wp = ple_w_proj[i].astype(BF16)
        yp = _channel_and_ple(yp, p_prompt[i].reshape(bp * s, -1), w1, w3, w2, wg, wp, ln_gain[i, 1], ln_bias[i, 1])
        ys = _channel_and_ple(ys, p_sample[i].reshape(bs * t, -1), w1, w3, w2, wg, wp, ln_gain[i, 1], ln_bias[i, 1])

    def stack(items, idx):
        return jnp.stack([it[idx] for it in items])

    return (yp.reshape(bp, s, d), ys.reshape(bs, t, d),
            stack(a_new_p, 0), stack(a_new_p, 1), stack(a_new_p, 2),
            stack(b_new_p, 0), stack(b_new_p, 1),
            stack(a_new_s, 0), stack(a_new_s, 1), stack(a_new_s, 2),
            stack(b_new_s, 0), stack(b_new_s, 1))
```
